```python
import jax, jax.numpy as jnp
from jax import lax
import numpy as np

D_MODEL = 1024
BATCH = 2
SEQ = 8192
DEPTH = 4
DEC_BATCH = 128
DEC_SEQ = 1
PAST_LEN = 2048
PAGE_SIZE = 128

N_MIXERS = 4
N_CONV_LAYERS = (DEPTH + 3) // N_MIXERS
N_POOL_LAYERS = (DEPTH + 2) // N_MIXERS
N_ATTN_LAYERS = (DEPTH + 1) // N_MIXERS
N_RWKV_LAYERS = DEPTH // N_MIXERS

RMS_EPS = 1e-6
LN_EPS = 1e-5
D_FF = 4 * D_MODEL

D_CONV = D_MODEL
CONV_WIDTH = 31

POOL_WINDOWS = (2, 4, 8, 16)
N_POOL_GROUPS = len(POOL_WINDOWS)
POOL_GROUP = D_MODEL // N_POOL_GROUPS
POOL_BUF = max(POOL_WINDOWS) - 1

HEAD_DIM = 64
N_HEADS_ATTN = D_MODEL // HEAD_DIM
Q_BLOCK = 128
SB_SCALE = HEAD_DIM ** -0.5
SB_BIAS_INIT = -6.0

N_HEADS_RWKV = D_MODEL // HEAD_DIM
D_DECAY_LORA = max(32, int(round(1.8 * D_MODEL ** 0.5 / 32)) * 32)
D_AAA_LORA = max(32, int(round(1.8 * D_MODEL ** 0.5 / 32)) * 32)
D_GATE_LORA = max(32, int(round(0.6 * D_MODEL ** 0.8 / 32)) * 32)
GN_EPS = 1e-5 * HEAD_DIM

kernel_name = "hybrid_conv_pool_stickbreak_rwkv7_step"


def rms_norm(x, g):
    xf = x.astype(jnp.float32)
    y = xf * lax.rsqrt(jnp.mean(xf * xf, axis=-1, keepdims=True) + RMS_EPS)
    return (y * g.astype(jnp.float32)).astype(x.dtype)


def layer_norm(x, g, b, eps):
    xf = x.astype(jnp.float32)
    xc = xf - jnp.mean(xf, axis=-1, keepdims=True)
    var = jnp.mean(xc * xc, axis=-1, keepdims=True)
    return (xc * lax.rsqrt(var + eps) * g.astype(jnp.float32) + b.astype(jnp.float32)).astype(x.dtype)


def sq_relu_mlp(h, w_up, w_down):
    return jnp.square(jax.nn.relu(h @ w_up)) @ w_down


def conv_mixer(xn, buf, w_in, dw, dw_b, ln_g, ln_b, w_out):
    u = xn @ w_in
    u = u[..., :D_CONV] * jax.nn.sigmoid(u[..., D_CONV:])
    up = jnp.concatenate([buf.astype(u.dtype), u], axis=1)
    h = lax.conv_general_dilated(up, dw.astype(u.dtype)[:, None, :], window_strides=(1,), padding='VALID',
                                 dimension_numbers=('NWC', 'WIO', 'NWC'), feature_group_count=D_CONV)
    h = jax.nn.silu(layer_norm(h + dw_b, ln_g, ln_b, LN_EPS))
    return h @ w_out, up[:, -(CONV_WIDTH - 1):]


def pool_mixer(xn, buf, start_pos, w_grp, scale):
    B, T, D = xn.shape
    xcat = jnp.concatenate([buf.astype(xn.dtype), xn], axis=1)
    xf = xcat.astype(jnp.float32)
    cs = jnp.cumsum(xf, axis=1)
    cs = jnp.concatenate([jnp.zeros_like(cs[:, :1]), cs], axis=1)
    pos = start_pos + jnp.arange(T)
    means = []
    for g, w in enumerate(POOL_WINDOWS):
        c = slice(g * POOL_GROUP, (g + 1) * POOL_GROUP)
        win = cs[:, POOL_BUF + 1:, c] - cs[:, POOL_BUF + 1 - w:POOL_BUF + 1 - w + T, c]
        cnt = jnp.minimum(w, pos + 1).astype(jnp.float32)[None, :, None]
        means.append(win / cnt)
    d = (jnp.concatenate(means, axis=-1) - xf[:, POOL_BUF:]).astype(xn.dtype)
    y = jnp.einsum('btgc,gce->btge', d.reshape(B, T, N_POOL_GROUPS, POOL_GROUP), w_grp)
    return y.reshape(B, T, D) * scale, xcat[:, -POOL_BUF:]


def sb_weights(z, mask):
    log_beta = jax.nn.log_sigmoid(z)
    log_keep = jnp.where(mask, log_beta - z, 0.0)
    between = lax.cumsum(log_keep, axis=z.ndim - 1, reverse=True) - log_keep
    return jnp.where(mask, jnp.exp(log_beta + between), 0.0)


def sb_attn_prompt(q, k, v, bias):
    B, S = q.shape[:2]
    nb = S // Q_BLOCK
    q_blocks = q.reshape(B, nb, Q_BLOCK, N_HEADS_ATTN, HEAD_DIM).swapaxes(0, 1)
    k_pos = jnp.arange(S)
    b = bias.astype(jnp.float32)[None, :, None, None]

    def block(args):
        q_blk, start = args
        z = jnp.einsum('bqhd,bkhd->bhqk', q_blk, k, preferred_element_type=jnp.float32) * SB_SCALE + b
        mask = k_pos[None, :] < (start + jnp.arange(Q_BLOCK))[:, None]
        a = sb_weights(z, mask)
        return jnp.einsum('bhqk,bkhd->bqhd', a.astype(v.dtype), v)

    o = lax.map(block, (q_blocks, jnp.arange(nb) * Q_BLOCK))
    return o.swapaxes(0, 1).reshape(B, S, N_HEADS_ATTN, HEAD_DIM)


def sb_attn_sample(q, k, v, bias, cache_k, cache_v, page_table, j):
    B, T = q.shape[:2]
    past = page_table.shape[1] * PAGE_SIZE
    k_past = cache_k[j][page_table].reshape(B, past, N_HEADS_ATTN, HEAD_DIM)
    v_past = cache_v[j][page_table].reshape(B, past, N_HEADS_ATTN, HEAD_DIM)
    z = jnp.concatenate([
        jnp.einsum('bqhd,bkhd->bhqk', q, k_past.astype(q.dtype), preferred_element_type=jnp.float32),
        jnp.einsum('bqhd,bkhd->bhqk', q, k, preferred_element_type=jnp.float32)], axis=-1) * SB_SCALE
    z = z + bias.astype(jnp.float32)[None, :, None, None]
    q_pos = past + jnp.arange(T)
    k_pos = jnp.arange(past + T)
    a = sb_weights(z, k_pos[None, :] < q_pos[:, None]).astype(v.dtype)
    return (jnp.einsum('bhqk,bkhd->bqhd', a[..., :past], v_past.astype(v.dtype))
            + jnp.einsum('bhqk,bkhd->bqhd', a[..., past:], v))


def attn_mixer(xn, kv_past, j, w_qkv, w_o, sb_bias):
    B, T, D = xn.shape
    qkv = (xn @ w_qkv).reshape(B, T, 3, N_HEADS_ATTN, HEAD_DIM)
    q, k, v = qkv[:, :, 0], qkv[:, :, 1], qkv[:, :, 2]
    if kv_past is None:
        o = sb_attn_prompt(q, k, v, sb_bias)
    else:
        o = sb_attn_sample(q, k, v, sb_bias, kv_past[0], kv_past[1], kv_past[2], j)
    return o.reshape(B, T, D) @ w_o, k, v


def wkv_scan(r, w, k, v, a, b, S0):
    xs = tuple(t.astype(jnp.float32).swapaxes(0, 1) for t in (r, w, k, v, a, b))

    def step(S, inp):
        r_t, w_t, k_t, v_t, a_t, b_t = inp
        sa = jnp.einsum('bhvk,bhk->bhv', S, a_t)
        S = S * w_t[:, :, None, :] + sa[..., None] * b_t[:, :, None, :] + v_t[..., None] * k_t[:, :, None, :]
        return S, jnp.einsum('bhvk,bhk->bhv', S, r_t)

    S, o = lax.scan(step, S0.astype(jnp.float32), xs)
    return o.swapaxes(0, 1), S


def rwkv_mixer(xn, shift_prev, S0, mix, w_r, w_k, w_v, w_o, w0, w1, w2, a0, a1, a2, g1, g2,
               k_k, k_a, r_k, ln_g, ln_b):
    B, T, D = xn.shape
    heads = lambda t: t.reshape(B, T, N_HEADS_RWKV, HEAD_DIM)
    x_prev = jnp.concatenate([shift_prev[:, None].astype(xn.dtype), xn[:, :-1]], axis=1)
    xx = x_prev - xn
    xr, xw, xk, xv, xa, xg = (xn + xx * mix[c] for c in range(6))
    r = xr @ w_r
    w = -jax.nn.softplus(-(w0 + jnp.tanh(xw @ w1) @ w2)) - 0.5
    k = xk @ w_k
    v = xv @ w_v
    a = jax.nn.sigmoid(a0 + (xa @ a1) @ a2)
    g = jax.nn.sigmoid(xg @ g1) @ g2
    kk = heads(k * k_k).astype(jnp.float32)
    kk = kk / jnp.maximum(jnp.sqrt(jnp.sum(kk * kk, axis=-1, keepdims=True)), 1e-12)
    k = k * (1 + (a - 1) * k_a)
    decay = jnp.exp(-jnp.exp(w.astype(jnp.float32)))
    o, S = wkv_scan(heads(r), heads(decay), heads(k), heads(v), -kk, kk * heads(a).astype(jnp.float32), S0)
    o = layer_norm(o, ln_g.reshape(N_HEADS_RWKV, HEAD_DIM), ln_b.reshape(N_HEADS_RWKV, HEAD_DIM), GN_EPS)
    bonus = jnp.sum((heads(r) * heads(k)).astype(jnp.float32) * r_k, axis=-1, keepdims=True)
    o = o + bonus * heads(v).astype(jnp.float32)
    y = (o.reshape(B, T, D).astype(xn.dtype) * g) @ w_o
    return y, S.astype(xn.dtype), xn[:, -1]


def trunk(x, start_pos, conv_buf, pool_buf, kv_past, wkv0, shift0, norms, mlp_p, conv_p, pool_p, attn_p, rwkv_p):
    norm_mix, norm_mlp, norm_final = norms
    new_conv, new_pool, new_k, new_v, new_wkv, new_shift = [], [], [], [], [], []
    for i in range(DEPTH):
        kind, j = i % N_MIXERS, i // N_MIXERS
        h = rms_norm(x, norm_mix[i])
        if kind == 0:
            y, buf = conv_mixer(h, conv_buf[j], *[p[j] for p in conv_p])
            new_conv.append(buf)
        elif kind == 1:
            y, buf = pool_mixer(h, pool_buf[j], start_pos, *[p[j] for p in pool_p])
            new_pool.append(buf)
        elif kind == 2:
            y, k, v = attn_mixer(h, kv_past, j, *[p[j] for p in attn_p])
            new_k.append(k)
            new_v.append(v)
        else:
            y, S, sh = rwkv_mixer(h, shift0[j], wkv0[j], *[p[j] for p in rwkv_p])
            new_wkv.append(S)
            new_shift.append(sh)
        x = x + y
        h = rms_norm(x, norm_mlp[i])
        x = x + sq_relu_mlp(h, mlp_p[0][i], mlp_p[1][i])
    return (rms_norm(x, norm_final), jnp.stack(new_conv), jnp.stack(new_pool), jnp.stack(new_k),
            jnp.stack(new_v), jnp.stack(new_wkv), jnp.stack(new_shift))


def setup_inputs(seed: int = 0) -> dict:
    key = jax.random.key(seed)
    keys = jax.random.split(key, 64)
    counter = [0]

    def nk():
        counter[0] += 1
        return keys[counter[0] - 1]

    f32 = jnp.float32
    nrm = lambda shape, s=1.0: jax.random.normal(nk(), shape, f32) * s
    gain = lambda shape: 1.0 + nrm(shape, 0.1)
    D = D_MODEL
    n_pages = PAST_LEN // PAGE_SIZE
    n_phys = (5 * DEC_BATCH * n_pages + 3) // 4
    page_table = jax.random.permutation(nk(), n_phys)[:DEC_BATCH * n_pages].reshape(DEC_BATCH, n_pages).astype(jnp.int32)
    return {
        "x_prompt": nrm((BATCH, SEQ, D)),
        "x_sample": nrm((DEC_BATCH, DEC_SEQ, D)),
        "cache_conv": nrm((N_CONV_LAYERS, DEC_BATCH, CONV_WIDTH - 1, D_CONV), 0.5),
        "cache_pool": nrm((N_POOL_LAYERS, DEC_BATCH, POOL_BUF, D)),
        "cache_k": nrm((N_ATTN_LAYERS, n_phys, PAGE_SIZE, N_HEADS_ATTN, HEAD_DIM)),
        "cache_v": nrm((N_ATTN_LAYERS, n_phys, PAGE_SIZE, N_HEADS_ATTN, HEAD_DIM)),
        "page_table": page_table,
        "state_wkv": nrm((N_RWKV_LAYERS, DEC_BATCH, N_HEADS_RWKV, HEAD_DIM, HEAD_DIM)),
        "state_shift": nrm((N_RWKV_LAYERS, DEC_BATCH, D)),
        "norm_mix": gain((DEPTH, D)),
        "norm_mlp": gain((DEPTH, D)),
        "norm_final": gain((D,)),
        "mlp_w_up": nrm((DEPTH, D, D_FF), D ** -0.5),
        "mlp_w_down": nrm((DEPTH, D_FF, D), D_FF ** -0.5),
        "conv_w_in": nrm((N_CONV_LAYERS, D, 2 * D_CONV), D ** -0.5),
        "conv_dw": nrm((N_CONV_LAYERS, CONV_WIDTH, D_CONV), CONV_WIDTH ** -0.5),
        "conv_dw_b": nrm((N_CONV_LAYERS, D_CONV), 0.01),
        "conv_ln_g": gain((N_CONV_LAYERS, D_CONV)),
        "conv_ln_b": nrm((N_CONV_LAYERS, D_CONV), 0.01),
        "conv_w_out": nrm((N_CONV_LAYERS, D_CONV, D), D_CONV ** -0.5),
        "pool_w": nrm((N_POOL_LAYERS, N_POOL_GROUPS, POOL_GROUP, POOL_GROUP), POOL_GROUP ** -0.5),
        "pool_scale": gain((N_POOL_LAYERS, D)),
        "attn_w_qkv": nrm((N_ATTN_LAYERS, D, 3 * D), D ** -0.5),
        "attn_w_o": nrm((N_ATTN_LAYERS, D, D), D ** -0.5),
        "attn_sb_bias": SB_BIAS_INIT + nrm((N_ATTN_LAYERS, N_HEADS_ATTN), 0.5),
        "rw_mix": jax.random.uniform(nk(), (N_RWKV_LAYERS, 6, D), f32),
        "rw_w_r": nrm((N_RWKV_LAYERS, D, D), D ** -0.5),
        "rw_w_k": nrm((N_RWKV_LAYERS, D, D), D ** -0.5),
        "rw_w_v": nrm((N_RWKV_LAYERS, D, D), D ** -0.5),
        "rw_w_o": nrm((N_RWKV_LAYERS, D, D), D ** -0.5),
        "rw_w0": nrm((N_RWKV_LAYERS, D), 0.5),
        "rw_w1": nrm((N_RWKV_LAYERS, D, D_DECAY_LORA), D ** -0.5),
        "rw_w2": nrm((N_RWKV_LAYERS, D_DECAY_LORA, D), 0.5 * D_DECAY_LORA ** -0.5),
        "rw_a0": nrm((N_RWKV_LAYERS, D), 0.1),
        "rw_a1": nrm((N_RWKV_LAYERS, D, D_AAA_LORA), D ** -0.5),
        "rw_a2": nrm((N_RWKV_LAYERS, D_AAA_LORA, D), 0.5 * D_AAA_LORA ** -0.5),
        "rw_g1": nrm((N_RWKV_LAYERS, D, D_GATE_LORA), D ** -0.5),
        "rw_g2": nrm((N_RWKV_LAYERS, D_GATE_LORA, D), D_GATE_LORA ** -0.5),
        "rw_k_k": 0.85 + nrm((N_RWKV_LAYERS, D), 0.1),
        "rw_k_a": gain((N_RWKV_LAYERS, D)),
        "rw_r_k": nrm((N_RWKV_LAYERS, N_HEADS_RWKV, HEAD_DIM), 0.1),
        "rw_ln_g": gain((N_RWKV_LAYERS, D)),
        "rw_ln_b": nrm((N_RWKV_LAYERS, D), 0.01),
    }


def reference(x_prompt, x_sample, cache_conv, cache_pool, cache_k, cache_v, page_table, state_wkv, state_shift,
              norm_mix, norm_mlp, norm_final, mlp_w_up, mlp_w_down,
              conv_w_in, conv_dw, conv_dw_b, conv_ln_g, conv_ln_b, conv_w_out,
              pool_w, pool_scale, attn_w_qkv, attn_w_o, attn_sb_bias,
              rw_mix, rw_w_r, rw_w_k, rw_w_v, rw_w_o, rw_w0, rw_w1, rw_w2, rw_a0, rw_a1, rw_a2,
              rw_g1, rw_g2, rw_k_k, rw_k_a, rw_r_k, rw_ln_g, rw_ln_b):
    norms = (norm_mix, norm_mlp, norm_final)
    mlp_p = (mlp_w_up, mlp_w_down)
    conv_p = (conv_w_in, conv_dw, conv_dw_b, conv_ln_g, conv_ln_b, conv_w_out)
    pool_p = (pool_w, pool_scale)
    attn_p = (attn_w_qkv, attn_w_o, attn_sb_bias)
    rwkv_p = (rw_mix, rw_w_r, rw_w_k, rw_w_v, rw_w_o, rw_w0, rw_w1, rw_w2, rw_a0, rw_a1, rw_a2,
              rw_g1, rw_g2, rw_k_k, rw_k_a, rw_r_k, rw_ln_g, rw_ln_b)

    B, dt = x_prompt.shape[0], x_prompt.dtype
    yp, conv_p_new, pool_p_new, k_p, v_p, wkv_p, shift_p = trunk(
        x_prompt, 0,
        jnp.zeros((N_CONV_LAYERS, B, CONV_WIDTH - 1, D_CONV), dt),
        jnp.zeros((N_POOL_LAYERS, B, POOL_BUF, D_MODEL), dt),
        None,
        jnp.zeros((N_RWKV_LAYERS, B, N_HEADS_RWKV, HEAD_DIM, HEAD_DIM), dt),
        jnp.zeros((N_RWKV_LAYERS, B, D_MODEL), dt),
        norms, mlp_p, conv_p, pool_p, attn_p, rwkv_p)

    past_len = page_table.shape[1] * PAGE_SIZE
    ys, conv_s_new, pool_s_new, k_s, v_s, wkv_s, shift_s = trunk(
        x_sample, past_len, cache_conv, cache_pool, (cache_k, cache_v, page_table), state_wkv, state_shift,
        norms, mlp_p, conv_p, pool_p, attn_p, rwkv_p)

    return (yp, ys, conv_p_new, conv_s_new, pool_p_new, pool_s_new, k_p, v_p, k_s, v_s,
            wkv_p, wkv_s, shift_p, shift_s)
```

```python
import functools

import jax
import jax.numpy as jnp
from jax import lax
from jax.experimental import pallas as pl
from jax.experimental.pallas import tpu as pltpu

F32 = jnp.float32
BF16 = jnp.bfloat16

D_MODEL = 1024
D_FF = 4 * D_MODEL
HEAD_DIM = 64
N_HEADS = D_MODEL // HEAD_DIM
CONV_WIDTH = 31
POOL_WINDOWS = (2, 4, 8, 16)
POOL_GROUP = D_MODEL // len(POOL_WINDOWS)
POOL_BUF = max(POOL_WINDOWS) - 1
PAGE_SIZE = 128
RMS_EPS = 1e-6
LN_EPS = 1e-5
GN_EPS = 1e-5 * HEAD_DIM
SB_SCALE = HEAD_DIM ** -0.5

LANES = 128
MXU_DIM = 256
VMEM_LIMIT = 48 << 20


def _params(sem, vmem=VMEM_LIMIT):
    return pltpu.CompilerParams(dimension_semantics=sem, vmem_limit_bytes=vmem)


def _rms(x, g):
    return x * lax.rsqrt(jnp.mean(x * x, axis=-1, keepdims=True) + RMS_EPS) * g


def _dot(a, b):
    return jnp.dot(a, b, preferred_element_type=F32)


def _dot_nt(a, b):
    return lax.dot_general(a, b, (((1,), (1,)), ((), ())), preferred_element_type=F32)


def _dot_tn(a, b):
    return lax.dot_general(a, b, (((0,), (0,)), ((), ())), preferred_element_type=F32)


def _ones_where(cond):
    return jnp.where(cond, 1.0, 0.0).astype(BF16)


def _split2(x):
    hi = x.astype(BF16)
    lo = (x - hi.astype(F32)).astype(BF16)
    return hi, lo


def _dot_x2(x, w):
    hi, lo = _split2(x)
    return _dot(hi, w) + _dot(lo, w)


def _dot_x3(w, x):
    hi = x.astype(BF16)
    r1 = x - hi.astype(F32)
    mid = r1.astype(BF16)
    lo = (r1 - mid.astype(F32)).astype(BF16)
    return _dot(w, hi) + _dot(w, mid) + _dot(w, lo)


def _row_spec(tm, n):
    return pl.BlockSpec((tm, n), lambda i: (i, 0))


def _full_spec(shape):
    return pl.BlockSpec(shape, lambda *_: (0,) * len(shape))


def _row_tile(m, pref):
    return pref if m % pref == 0 else m


def _mlp_kernel(x_ref, g_ref, gf_ref, wu_ref, wd_ref, o_ref, xn_ref, *, nj, final_norm):
    j = pl.program_id(1)

    @pl.when(j == 0)
    def _():
        x = x_ref[...]
        xn_ref[...] = _rms(x, g_ref[...]).astype(BF16)
        o_ref[...] = x

    h = _dot(xn_ref[...], wu_ref[...])
    h = jnp.square(jnp.maximum(h, 0.0)).astype(BF16)
    o_ref[...] += _dot(h, wd_ref[...])

    if final_norm:
        @pl.when(j == nj - 1)
        def _():
            o_ref[...] = _rms(o_ref[...], gf_ref[...])


def mlp(x, g, w_up, w_down, g_final=None):
    m, d = x.shape
    tm = _row_tile(m, 512)
    tf = 1024
    nj = D_FF // tf
    final_norm = g_final is not None
    gf = g_final if final_norm else g
    return pl.pallas_call(
        functools.partial(_mlp_kernel, nj=nj, final_norm=final_norm),
        out_shape=jax.ShapeDtypeStruct((m, d), F32),
        grid=(m // tm, nj),
        in_specs=[
            pl.BlockSpec((tm, d), lambda i, j: (i, 0)),
            pl.BlockSpec((1, d), lambda i, j: (0, 0)),
            pl.BlockSpec((1, d), lambda i, j: (0, 0)),
            pl.BlockSpec((d, tf), lambda i, j: (0, j)),
            pl.BlockSpec((tf, d), lambda i, j: (j, 0)),
        ],
        out_specs=pl.BlockSpec((tm, d), lambda i, j: (i, 0)),
        scratch_shapes=[pltpu.VMEM((tm, d), BF16)],
        compiler_params=_params(("parallel", "arbitrary")),
        name="mlp",
    )(x, g.reshape(1, d), gf.reshape(1, d), w_up, w_down)


def _proj_res_kernel(a_ref, x_ref, w_ref, o_ref):
    o_ref[...] = x_ref[...] + _dot(a_ref[...].astype(BF16), w_ref[...])


def proj_res(a, x, w):
    m, d = x.shape
    tm = _row_tile(m, 512)
    return pl.pallas_call(
        _proj_res_kernel,
        out_shape=jax.ShapeDtypeStruct((m, d), F32),
        grid=(m // tm,),
        in_specs=[_row_spec(tm, a.shape[1]), _row_spec(tm, d), _full_spec(w.shape)],
        out_specs=_row_spec(tm, d),
        compiler_params=_params(("parallel",)),
        name="proj_res",
    )(a, x, w)


def _conv_in_kernel(x_ref, g_ref, wa_ref, wb_ref, o_ref):
    xn = _rms(x_ref[...], g_ref[...]).astype(BF16)
    o_ref[...] = _dot(xn, wa_ref[...]) * jax.nn.sigmoid(_dot(xn, wb_ref[...]))


def conv_in(x, g, w_a, w_b):
    m, d = x.shape
    tm = _row_tile(m, 512)
    return pl.pallas_call(
        _conv_in_kernel,
        out_shape=jax.ShapeDtypeStruct((m, d), F32),
        grid=(m // tm,),
        in_specs=[_row_spec(tm, d), _full_spec((1, d)), _full_spec(w_a.shape), _full_spec(w_b.shape)],
        out_specs=_row_spec(tm, d),
        compiler_params=_params(("parallel",)),
        name="conv_in",
    )(x, g.reshape(1, d), w_a, w_b)


def _ln_silu_proj(h, lng, lnb, w):
    mu = jnp.mean(h, axis=-1, keepdims=True)
    hc = h - mu
    var = jnp.mean(hc * hc, axis=-1, keepdims=True)
    hn = hc * lax.rsqrt(var + LN_EPS) * lng + lnb
    hn = hn * jax.nn.sigmoid(hn)
    return _dot(hn.astype(BF16), w)


CONV_HALO = 32
CONV_ROWS = 16


def _conv_seq_kernel(u_ref, halo_ref, x_ref, dw_ref, dwb_ref, lng_ref, lnb_ref, w_ref, o_ref,
                     up_ref, h_ref, *, tm):
    i = pl.program_id(1)
    up_ref[0:CONV_HALO, :] = jnp.where(i > 0, halo_ref[0], 0.0)
    up_ref[CONV_HALO:, :] = u_ref[0]
    lead = CONV_HALO - (CONV_WIDTH - 1)
    for c in range(tm // CONV_ROWS):
        r0 = c * CONV_ROWS
        acc = jnp.zeros((CONV_ROWS, D_MODEL), F32)
        for j in range(CONV_WIDTH):
            acc = acc + up_ref[r0 + lead + j:r0 + lead + j + CONV_ROWS, :] * dw_ref[j:j + 1, :]
        h_ref[r0:r0 + CONV_ROWS, :] = acc
    h = h_ref[...] + dwb_ref[...]
    o_ref[0] = x_ref[0] + _ln_silu_proj(h, lng_ref[...], lnb_ref[...], w_ref[...])


def conv_seq(u, x, dw, dwb, lng, lnb, w_out):
    b, s, d = u.shape
    tm = 128
    nt = s // tm
    hb = tm // CONV_HALO
    vec = lambda v: v.reshape(1, d)
    return pl.pallas_call(
        functools.partial(_conv_seq_kernel, tm=tm),
        out_shape=jax.ShapeDtypeStruct((b, s, d), F32),
        grid=(b, nt),
        in_specs=[
            pl.BlockSpec((1, tm, d), lambda bi, i: (bi, i, 0)),
            pl.BlockSpec((1, CONV_HALO, d), lambda bi, i: (bi, jnp.maximum(i * hb - 1, 0), 0)),
            pl.BlockSpec((1, tm, d), lambda bi, i: (bi, i, 0)),
            pl.BlockSpec((CONV_WIDTH, d), lambda bi, i: (0, 0)),
            pl.BlockSpec((1, d), lambda bi, i: (0, 0)),
            pl.BlockSpec((1, d), lambda bi, i: (0, 0)),
            pl.BlockSpec((1, d), lambda bi, i: (0, 0)),
            pl.BlockSpec((d, d), lambda bi, i: (0, 0)),
        ],
        out_specs=pl.BlockSpec((1, tm, d), lambda bi, i: (bi, i, 0)),
        scratch_shapes=[pltpu.VMEM((tm + CONV_HALO, d), F32), pltpu.VMEM((tm, d), F32)],
        compiler_params=_params(("parallel", "arbitrary")),
        name="conv_seq",
    )(u, u, x, dw, vec(dwb), vec(lng), vec(lnb), w_out)


def _conv_step_kernel(u_ref, buf_ref, x_ref, dw_ref, dwb_ref, lng_ref, lnb_ref, w_ref, o_ref):
    nb = CONV_WIDTH - 1
    h = jnp.sum(buf_ref[...] * dw_ref[0:nb, :][None], axis=1)
    h = h + u_ref[...] * dw_ref[nb:nb + 1, :] + dwb_ref[...]
    o_ref[...] = x_ref[...] + _ln_silu_proj(h, lng_ref[...], lnb_ref[...], w_ref[...])


def conv_step(u, buf, x, dw, dwb, lng, lnb, w_out):
    b, d = u.shape
    tb = 32
    vec = lambda v: v.reshape(1, d)
    return pl.pallas_call(
        _conv_step_kernel,
        out_shape=jax.ShapeDtypeStruct((b, d), F32),
        grid=(b // tb,),
        in_specs=[
            _row_spec(tb, d),
            pl.BlockSpec((tb, CONV_WIDTH - 1, d), lambda i: (i, 0, 0)),
            _row_spec(tb, d),
            _full_spec((CONV_WIDTH, d)), _full_spec((1, d)), _full_spec((1, d)), _full_spec((1, d)),
            _full_spec((d, d)),
        ],
        out_specs=_row_spec(tb, d),
        compiler_params=_params(("parallel",)),
        name="conv_step",
    )(u, buf, x, dw, vec(dwb), vec(lng), vec(lnb), w_out)


POOL_HALO = 16


def _pool_project(d, w_ref, scale):
    ys = []
    for gi in range(len(POOL_WINDOWS)):
        dg = d[:, gi * POOL_GROUP:(gi + 1) * POOL_GROUP].astype(BF16)
        ys.append(_dot(dg, w_ref[gi]))
    return jnp.concatenate(ys, axis=-1) * scale


def _pool_seq_kernel(x_ref, halo_ref, g_ref, w_ref, sc_ref, o_ref, tail_ref, xc_ref, *, tm, nt):
    i = pl.program_id(1)
    g = g_ref[...]
    xn = _rms(x_ref[0], g)
    xc_ref[0:POOL_HALO, :] = jnp.where(i > 0, _rms(halo_ref[0], g), 0.0)
    xc_ref[POOL_HALO:, :] = xn
    pos = i * tm + lax.broadcasted_iota(jnp.int32, (tm, 1), 0)
    means = []
    for gi, w in enumerate(POOL_WINDOWS):
        c = slice(gi * POOL_GROUP, (gi + 1) * POOL_GROUP)
        win = xn[:, c]
        for k in range(1, w):
            win = win + xc_ref[POOL_HALO - k:POOL_HALO - k + tm, c]
        cnt = jnp.minimum(w, pos + 1).astype(F32)
        means.append(win / cnt)
    d = jnp.concatenate(means, axis=-1) - xn
    o_ref[0] = x_ref[0] + _pool_project(d, w_ref, sc_ref[...])

    @pl.when(i == nt - 1)
    def _():
        tail_ref[0] = xn[tm - POOL_HALO:, :]


def pool_seq(x, g, w_grp, scale):
    b, s, d = x.shape
    tm = 256
    nt = s // tm
    hb = tm // POOL_HALO
    return pl.pallas_call(
        functools.partial(_pool_seq_kernel, tm=tm, nt=nt),
        out_shape=(jax.ShapeDtypeStruct((b, s, d), F32), jax.ShapeDtypeStruct((b, POOL_HALO, d), F32)),
        grid=(b, nt),
        in_specs=[
            pl.BlockSpec((1, tm, d), lambda bi, i: (bi, i, 0)),
            pl.BlockSpec((1, POOL_HALO, d), lambda bi, i: (bi, jnp.maximum(i * hb - 1, 0), 0)),
            pl.BlockSpec((1, d), lambda bi, i: (0, 0)),
            pl.BlockSpec(w_grp.shape, lambda bi, i: (0, 0, 0)),
            pl.BlockSpec((1, d), lambda bi, i: (0, 0)),
        ],
        out_specs=(pl.BlockSpec((1, tm, d), lambda bi, i: (bi, i, 0)),
                   pl.BlockSpec((1, POOL_HALO, d), lambda bi, i: (bi, 0, 0))),
        scratch_shapes=[pltpu.VMEM((tm + POOL_HALO, d), F32)],
        compiler_params=_params(("parallel", "arbitrary")),
        name="pool_seq",
    )(x, x, g.reshape(1, d), w_grp, scale.reshape(1, d))


def _pool_step_kernel(x_ref, buf_ref, g_ref, msk_ref, icnt_ref, w_ref, sc_ref, o_ref, xn_ref):
    xn = _rms(x_ref[...], g_ref[...])
    win = xn + jnp.sum(buf_ref[...] * msk_ref[...][None], axis=1)
    d = win * icnt_ref[...] - xn
    o_ref[...] = x_ref[...] + _pool_project(d, w_ref, sc_ref[...])
    xn_ref[...] = xn


def pool_step(x, buf, start_pos, g, w_grp, scale):
    b, d = x.shape
    tb = 32
    rows = jnp.arange(POOL_BUF)[:, None]
    win_of_lane = jnp.repeat(jnp.asarray(POOL_WINDOWS), POOL_GROUP)[None, :]
    msk = (rows >= POOL_BUF - (win_of_lane - 1)).astype(F32)
    icnt = 1.0 / jnp.minimum(win_of_lane, start_pos + 1).astype(F32)
    return pl.pallas_call(
        _pool_step_kernel,
        out_shape=(jax.ShapeDtypeStruct((b, d), F32), jax.ShapeDtypeStruct((b, d), F32)),
        grid=(b // tb,),
        in_specs=[
            _row_spec(tb, d),
            pl.BlockSpec((tb, POOL_BUF, d), lambda i: (i, 0, 0)),
            _full_spec((1, d)), _full_spec((POOL_BUF, d)), _full_spec((1, d)),
            _full_spec(w_grp.shape), _full_spec((1, d)),
        ],
        out_specs=(_row_spec(tb, d), _row_spec(tb, d)),
        compiler_params=_params(("parallel",)),
        name="pool_step",
    )(x, buf, g.reshape(1, d), msk, icnt, w_grp, scale.reshape(1, d))


def _qkv_kernel(x_ref, g_ref, wq_ref, wk_ref, wv_ref, q_ref, k_ref, v_ref, kb_ref, vb_ref):
    xn = _rms(x_ref[...], g_ref[...]).astype(BF16)
    q_ref[...] = (_dot(xn, wq_ref[...]) * SB_SCALE).astype(BF16)
    k = _dot(xn, wk_ref[...])
    v = _dot(xn, wv_ref[...])
    k_ref[...] = k
    v_ref[...] = v
    kb_ref[...] = k.astype(BF16)
    vb_ref[...] = v.astype(BF16)


def qkv_proj(x, g, wq, wk, wv):
    m, d = x.shape
    tm = _row_tile(m, 512)
    sd = lambda dt: jax.ShapeDtypeStruct((m, d), dt)
    return pl.pallas_call(
        _qkv_kernel,
        out_shape=(sd(BF16), sd(F32), sd(F32), sd(BF16), sd(BF16)),
        grid=(m // tm,),
        in_specs=[_row_spec(tm, d), _full_spec((1, d)), _full_spec((d, d)), _full_spec((d, d)),
                  _full_spec((d, d))],
        out_specs=tuple(_row_spec(tm, d) for _ in range(5)),
        compiler_params=_params(("parallel",)),
        name="qkv_proj",
    )(x, g.reshape(1, d), wq, wk, wv)


def _softplus(z):
    return jnp.maximum(z, 0.0) + jnp.log(1.0 + jnp.exp(-jnp.abs(z)))


ATT_BLK = 256


def _sb_block(q, k, v, bias, c, acc, upper, diag_mask):
    z = _dot_nt(q, k) + bias
    sp = _softplus(z)
    lk = -sp
    if diag_mask is not None:
        lk = jnp.where(diag_mask, lk, 0.0)
    between = _dot(lk.astype(BF16), upper)
    p = jnp.exp(z - sp + between + c)
    if diag_mask is not None:
        p = jnp.where(diag_mask, p, 0.0)
    acc = acc + _dot(p.astype(BF16), v)
    c = c + jnp.sum(lk, axis=-1, keepdims=True)
    return c, acc


def _sb_prompt_kernel(bias_ref, q_ref, k_ref, v_ref, o_ref):
    hp = pl.program_id(1)
    i = pl.program_id(2)
    t = ATT_BLK
    row = lax.broadcasted_iota(jnp.int32, (t, t), 0)
    col = lax.broadcasted_iota(jnp.int32, (t, t), 1)
    upper = _ones_where(row > col)
    diag_mask = col < row
    outs = []
    for hh in range(2):
        ls = slice(hh * HEAD_DIM, (hh + 1) * HEAD_DIM)
        bias = bias_ref[2 * hp + hh]
        q = q_ref[0, :, ls]
        k0 = pl.multiple_of(i * t, t)
        c, acc = _sb_block(q, k_ref[0, pl.ds(k0, t), ls], v_ref[0, pl.ds(k0, t), ls], bias,
                           jnp.zeros((t, 1), F32), jnp.zeros((t, HEAD_DIM), F32), upper, diag_mask)

        def body(n, carry):
            ks = pl.multiple_of((i - 1 - n) * t, t)
            return _sb_block(q, k_ref[0, pl.ds(ks, t), ls], v_ref[0, pl.ds(ks, t), ls], bias,
                             carry[0], carry[1], upper, None)

        c, acc = lax.fori_loop(0, i, body, (c, acc))
        outs.append(acc)
    o_ref[0] = jnp.concatenate(outs, axis=-1)


def sb_attn_prompt(q, k, v, bias):
    b, s, d = q.shape
    t = ATT_BLK
    return pl.pallas_call(
        _sb_prompt_kernel,
        out_shape=jax.ShapeDtypeStruct((b, s, d), F32),
        grid_spec=pltpu.PrefetchScalarGridSpec(
            num_scalar_prefetch=0,
            grid=(b, N_HEADS // 2, s // t),
            in_specs=[
                pl.BlockSpec(memory_space=pltpu.SMEM),
                pl.BlockSpec((1, t, LANES), lambda bi, hp, i: (bi, i, hp)),
                pl.BlockSpec((1, s, LANES), lambda bi, hp, i: (bi, 0, hp)),
                pl.BlockSpec((1, s, LANES), lambda bi, hp, i: (bi, 0, hp)),
            ],
            out_specs=pl.BlockSpec((1, t, LANES), lambda bi, hp, i: (bi, i, hp)),
        ),
        compiler_params=_params(("parallel", "parallel", "arbitrary")),
        name="sb_attn_prompt",
    )(bias, q, k, v)


def _sb_decode_kernel(pt_ref, q_ref, kn_ref, vn_ref, bias_ref, sel_ref, exp_ref, kp_ref, vp_ref, o_ref,
                      c_ref, acc_ref, *, n_pages):
    del pt_ref
    p = pl.program_id(1)
    q = q_ref[0]
    bias = bias_ref[...]
    sel = sel_ref[...]
    expand = exp_ref[...]

    @pl.when(p == 0)
    def _():
        zn = _dot((kn_ref[0] * q).astype(BF16), sel) + bias
        q_pos = n_pages * PAGE_SIZE
        visible = (q_pos + lax.broadcasted_iota(jnp.int32, zn.shape, 0)) < q_pos
        spn = _softplus(zn)
        c_ref[...] = jnp.where(visible, -spn, 0.0)
        an = jnp.where(visible, jnp.exp(zn - spn), 0.0)
        acc_ref[...] = jnp.zeros_like(acc_ref)
        acc_ref[0:1, :] = _dot(an.astype(BF16), expand) * vn_ref[0]

    t = PAGE_SIZE
    z = _dot((kp_ref[0] * q).astype(BF16), sel) + bias
    sp = _softplus(z)
    lk = -sp
    row = lax.broadcasted_iota(jnp.int32, (t, t), 0)
    col = lax.broadcasted_iota(jnp.int32, (t, t), 1)
    later = _ones_where(col > row)
    between = _dot(later, lk.astype(BF16))
    a = jnp.exp(z - sp + between + c_ref[...])
    av = _dot(a.astype(BF16), expand) * vp_ref[0]
    acc_ref[...] += jnp.sum(av.reshape(t // 8, 8, D_MODEL), axis=0)
    c_ref[...] += jnp.sum(lk, axis=0, keepdims=True)

    @pl.when(p == n_pages - 1)
    def _():
        o_ref[0] = jnp.sum(acc_ref[...], axis=0, keepdims=True)


def sb_attn_decode(q, k_new, v_new, bias, cache_k, cache_v, page_table):
    b, d = q.shape
    n_pages = page_table.shape[1]
    lane = jnp.arange(d)[:, None] // HEAD_DIM
    sel = (lane == jnp.arange(LANES)[None, :]).astype(BF16)
    expand = sel.T
    bias_l = jnp.zeros((1, LANES), F32).at[0, :N_HEADS].set(bias)
    row3 = lambda a: a.reshape(b, 1, d)
    page = lambda bi, p, pt: (pt[bi * n_pages + (n_pages - 1 - p)], 0, 0)
    out = pl.pallas_call(
        functools.partial(_sb_decode_kernel, n_pages=n_pages),
        out_shape=jax.ShapeDtypeStruct((b, 1, d), F32),
        grid_spec=pltpu.PrefetchScalarGridSpec(
            num_scalar_prefetch=1,
            grid=(b, n_pages),
            in_specs=[
                pl.BlockSpec((1, 1, d), lambda bi, p, pt: (bi, 0, 0)),
                pl.BlockSpec((1, 1, d), lambda bi, p, pt: (bi, 0, 0)),
                pl.BlockSpec((1, 1, d), lambda bi, p, pt: (bi, 0, 0)),
                pl.BlockSpec((1, LANES), lambda bi, p, pt: (0, 0)),
                pl.BlockSpec((d, LANES), lambda bi, p, pt: (0, 0)),
                pl.BlockSpec((LANES, d), lambda bi, p, pt: (0, 0)),
                pl.BlockSpec((1, PAGE_SIZE, d), page),
                pl.BlockSpec((1, PAGE_SIZE, d), page),
            ],
            out_specs=pl.BlockSpec((1, 1, d), lambda bi, p, pt: (bi, 0, 0)),
            scratch_shapes=[pltpu.VMEM((1, LANES), F32), pltpu.VMEM((8, d), F32)],
        ),
        compiler_params=_params(("parallel", "arbitrary")),
        name="sb_attn_decode",
    )(page_table.reshape(-1), row3(q), row3(k_new), row3(v_new), bias_l, sel, expand, cache_k, cache_v)
    return out.reshape(b, d)


def _head_sum(x, sel, expand):
    return _dot_x2(_dot_x2(x, sel), expand)


def _rwkv_in_kernel(x_ref, prev_ref, g_ref, mix_ref, wr_ref, wk_ref, wv_ref, w0_ref, w1_ref, w2_ref,
                    a0_ref, a1_ref, a2_ref, g1_ref, g2_ref, kk_ref, ka_ref, sel_ref, exp_ref,
                    r_o, lw_o, k_o, v_o, kk_o, a_o, g_o, xn_o):
    xn = _rms(x_ref[...], g_ref[...])
    xx = prev_ref[...] - xn
    mixed = lambda c: (xn + xx * mix_ref[c:c + 1, :]).astype(BF16)
    xr, xw, xk, xv, xa, xg = (mixed(c) for c in range(6))
    r = _dot(xr, wr_ref[...])
    w = w0_ref[...] + _dot(jnp.tanh(_dot(xw, w1_ref[...])).astype(BF16), w2_ref[...])
    w = -_softplus(-w) - 0.5
    k = _dot(xk, wk_ref[...])
    v = _dot(xv, wv_ref[...])
    a = jax.nn.sigmoid(a0_ref[...] + _dot(_dot(xa, a1_ref[...]).astype(BF16), a2_ref[...]))
    g = _dot(jax.nn.sigmoid(_dot(xg, g1_ref[...])).astype(BF16), g2_ref[...])
    kk = k * kk_ref[...]
    nrm = jnp.sqrt(_head_sum(kk * kk, sel_ref[...], exp_ref[...]))
    kk = kk / jnp.maximum(nrm, 1e-12)
    r_o[...] = r
    lw_o[...] = -jnp.exp(w)
    k_o[...] = k * (1.0 + (a - 1.0) * ka_ref[...])
    v_o[...] = v
    kk_o[...] = kk
    a_o[...] = a
    g_o[...] = g
    xn_o[...] = xn


def _head_selectors():
    lane = jnp.arange(D_MODEL)[:, None] // HEAD_DIM
    sel = (lane == jnp.arange(LANES)[None, :]).astype(BF16)
    return sel, sel.T


def rwkv_in(x, x_prev_rows, g, p):
    m, d = x.shape
    tm = _row_tile(m, 256)
    sel, expand = _head_selectors()
    vec = lambda v: v.reshape(1, d)
    ws = [p["mix"], p["w_r"], p["w_k"], p["w_v"], vec(p["w0"]), p["w1"], p["w2"], vec(p["a0"]), p["a1"],
          p["a2"], p["g1"], p["g2"], vec(p["k_k"]), vec(p["k_a"]), sel, expand]
    sd = jax.ShapeDtypeStruct((m, d), F32)
    return pl.pallas_call(
        _rwkv_in_kernel,
        out_shape=(sd,) * 8,
        grid=(m // tm,),
        in_specs=[_row_spec(tm, d), _row_spec(tm, d), _full_spec((1, d))] + [_full_spec(w.shape) for w in ws],
        out_specs=tuple(_row_spec(tm, d) for _ in range(8)),
        compiler_params=_params(("parallel",)),
        name="rwkv_in",
    )(x, x_prev_rows, vec(g), *ws)


def _norm_shift_kernel(x_ref, halo_ref, g_ref, o_ref, *, tm):
    i = pl.program_id(1)
    g = g_ref[...]
    xn = _rms(x_ref[0], g)
    last = _rms(halo_ref[0], g)[7:8, :]
    first = jnp.where(i > 0, last, 0.0)
    rolled = pltpu.roll(xn, 1, axis=0)
    row = lax.broadcasted_iota(jnp.int32, (tm, 1), 0)
    o_ref[0] = jnp.where(row == 0, first, rolled)


def norm_shift(x, g):
    b, s, d = x.shape
    tm = 512
    hb = tm // 8
    return pl.pallas_call(
        functools.partial(_norm_shift_kernel, tm=tm),
        out_shape=jax.ShapeDtypeStruct((b, s, d), F32),
        grid=(b, s // tm),
        in_specs=[
            pl.BlockSpec((1, tm, d), lambda bi, i: (bi, i, 0)),
            pl.BlockSpec((1, 8, d), lambda bi, i: (bi, jnp.maximum(i * hb - 1, 0), 0)),
            pl.BlockSpec((1, d), lambda bi, i: (0, 0)),
        ],
        out_specs=pl.BlockSpec((1, tm, d), lambda bi, i: (bi, i, 0)),
        compiler_params=_params(("parallel", "arbitrary")),
        name="norm_shift",
    )(x, x, g.reshape(1, d))


WKV_CHUNK = 64


def _wkv_seq_kernel(r_ref, lw_ref, k_ref, v_ref, kk_ref, a_ref, o_ref, s_ref, h_ref, *, tb, nt):
    i = pl.program_id(1)
    c = WKV_CHUNK
    dh = HEAD_DIM

    @pl.when(i == 0)
    def _():
        h_ref[...] = jnp.zeros_like(h_ref)

    row = lax.broadcasted_iota(jnp.int32, (c, c), 0)
    col = lax.broadcasted_iota(jnp.int32, (c, c), 1)
    tri_incl = _ones_where(col <= row)
    eye = row == col
    row2 = lax.broadcasted_iota(jnp.int32, (2 * c, 2 * c), 0)
    col2 = lax.broadcasted_iota(jnp.int32, (2 * c, 2 * c), 1)
    colm = jnp.where(col2 < c, col2, col2 - c)
    gmask = colm < jnp.where(row2 < c, row2, row2 - c + 1)

    def chunk(n, carry):
        t0 = pl.multiple_of(n * c, c)
        sl = pl.ds(t0, c)
        lw = lw_ref[0, sl, :]
        cl = _dot_x3(tri_incl, lw)
        kk = kk_ref[0, sl, :]
        cl_end = cl[c - 1:c, :]
        g_end = jnp.exp(cl_end)
        inv = jnp.exp(-cl)
        at = (-kk * jnp.exp(cl - lw)).astype(BF16)
        rt_f = r_ref[0, sl, :] * jnp.exp(cl)
        rt = rt_f.astype(BF16)
        b = kk * a_ref[0, sl, :]
        k = k_ref[0, sl, :]
        bt = (b * inv).astype(BF16)
        kt = (k * inv).astype(BF16)
        to_end = jnp.exp(cl_end - cl)
        be = (b * to_end).astype(BF16)
        ke = (k * to_end).astype(BF16)
        vb = v_ref[0, sl, :].astype(BF16)
        outs = []
        for h in range(N_HEADS):
            ls = slice(h * dh, (h + 1) * dh)
            at_h, rt_h = at[:, ls], rt[:, ls]
            gm = _dot_nt(jnp.concatenate([at_h, rt_h], axis=0), jnp.concatenate([bt[:, ls], kt[:, ls]], axis=0))
            gm = jnp.where(gmask, gm, 0.0)
            mab = gm[0:c, 0:c]
            mak = gm[0:c, c:]
            nbk = gm[c:, :].astype(BF16)
            pw = mab.astype(BF16)
            tm_ = jnp.where(eye, 1.0, mab)
            sq = mab
            for _ in range(5):
                sq = _dot(pw, pw)
                pw = sq.astype(BF16)
                tm_ = tm_ + _dot(tm_.astype(BF16), pw)
            v_h = vb[:, ls]
            mv = _dot(mak.astype(BF16), v_h)
            aw = _dot(tm_.astype(BF16), jnp.concatenate([at_h, mv.astype(BF16)], axis=1))
            zmat = jnp.concatenate(
                [aw.astype(BF16), jnp.concatenate([jnp.zeros((c, dh), BF16), v_h], axis=1)], axis=0)
            x1 = _dot(nbk, zmat)
            x2 = _dot_tn(jnp.concatenate([be[:, ls], ke[:, ls]], axis=0), zmat)
            dmat = jnp.where(eye, jnp.broadcast_to(g_end[:, ls], (c, dh)), 0.0)
            lhs = jnp.concatenate([rt_f[:, ls] + x1[:, 0:dh], dmat + x2[:, 0:dh]], axis=0)
            hs = h_ref[h]
            hi, lo = _split2(hs)
            lb = lhs.astype(BF16)
            res = _dot(lb, hi) + _dot(lb, lo)
            outs.append(res[0:c, :] + x1[:, dh:])
            h_ref[h] = res[c:, :] + x2[:, dh:]
        o_ref[0, sl, :] = jnp.concatenate(outs, axis=-1)
        return carry

    lax.fori_loop(0, tb // c, chunk, 0)

    @pl.when(i == nt - 1)
    def _():
        s_ref[0] = h_ref[...]


def wkv_seq(r, lw, k, v, kk, a):
    b, s, d = r.shape
    tb = 256
    nt = s // tb
    blk = pl.BlockSpec((1, tb, d), lambda bi, i: (bi, i, 0))
    return pl.pallas_call(
        functools.partial(_wkv_seq_kernel, tb=tb, nt=nt),
        out_shape=(jax.ShapeDtypeStruct((b, s, d), F32),
                   jax.ShapeDtypeStruct((b, N_HEADS, HEAD_DIM, HEAD_DIM), F32)),
        grid=(b, nt),
        in_specs=[blk] * 6,
        out_specs=(blk, pl.BlockSpec((1, N_HEADS, HEAD_DIM, HEAD_DIM), lambda bi, i: (bi, 0, 0, 0))),
        scratch_shapes=[pltpu.VMEM((N_HEADS, HEAD_DIM, HEAD_DIM), F32)],
        compiler_params=_params(("parallel", "arbitrary")),
        name="wkv_seq",
    )(r, lw, k, v, kk, a)


def _wkv_step_kernel(r_ref, lw_ref, k_ref, v_ref, kk_ref, a_ref, s0_ref, o_ref, s_ref, *, tb):
    dh = HEAD_DIM
    row = lax.broadcasted_iota(jnp.int32, (dh, dh), 0)
    col = lax.broadcasted_iota(jnp.int32, (dh, dh), 1)
    eye = row == col
    for bi in range(tb):
        rb = slice(bi, bi + 1)
        outs = []
        for h in range(N_HEADS):
            ls = slice(h * dh, (h + 1) * dh)
            s0 = s0_ref[bi, h]
            kk = kk_ref[rb, ls]
            w = jnp.exp(lw_ref[rb, ls])
            a_s = -kk
            b_s = kk * a_ref[rb, ls]
            sa = jnp.sum(s0 * a_s, axis=-1, keepdims=True)
            v_col = jnp.sum(jnp.where(eye, jnp.broadcast_to(v_ref[rb, ls], (dh, dh)), 0.0),
                            axis=-1, keepdims=True)
            s1 = s0 * w + sa * b_s + v_col * k_ref[rb, ls]
            s_ref[bi, h] = s1
            o_col = jnp.sum(s1 * r_ref[rb, ls], axis=-1, keepdims=True)
            outs.append(jnp.sum(jnp.where(eye, jnp.broadcast_to(o_col, (dh, dh)), 0.0), axis=0, keepdims=True))
        o_ref[rb, :] = jnp.concatenate(outs, axis=-1)


def wkv_step(r, lw, k, v, kk, a, s0):
    b, d = r.shape
    tb = 8
    st = pl.BlockSpec((tb, N_HEADS, HEAD_DIM, HEAD_DIM), lambda i: (i, 0, 0, 0))
    return pl.pallas_call(
        functools.partial(_wkv_step_kernel, tb=tb),
        out_shape=(jax.ShapeDtypeStruct((b, d), F32), jax.ShapeDtypeStruct(s0.shape, F32)),
        grid=(b // tb,),
        in_specs=[_row_spec(tb, d)] * 6 + [st],
        out_specs=(_row_spec(tb, d), st),
        compiler_params=_params(("parallel",)),
        name="wkv_step",
    )(r, lw, k, v, kk, a, s0)


def _rwkv_out_kernel(o_ref, r_ref, k_ref, v_ref, g_ref, x_ref, rk_ref, lng_ref, lnb_ref, sel_ref, exp_ref,
                     w_ref, y_ref):
    sel, expand = sel_ref[...], exp_ref[...]
    o = o_ref[...]
    mu = _head_sum(o, sel, expand) * (1.0 / HEAD_DIM)
    oc = o - mu
    var = _head_sum(oc * oc, sel, expand) * (1.0 / HEAD_DIM)
    on = oc * lax.rsqrt(var + GN_EPS) * lng_ref[...] + lnb_ref[...]
    bonus = _head_sum(r_ref[...] * k_ref[...] * rk_ref[...], sel, expand)
    on = on + bonus * v_ref[...]
    y_ref[...] = x_ref[...] + _dot((on * g_ref[...]).astype(BF16), w_ref[...])


def rwkv_out(o, r, k, v, g, x, r_k, ln_g, ln_b, w_o):
    m, d = x.shape
    tm = _row_tile(m, 256)
    sel, expand = _head_selectors()
    vec = lambda t: t.reshape(1, d)
    return pl.pallas_call(
        _rwkv_out_kernel,
        out_shape=jax.ShapeDtypeStruct((m, d), F32),
        grid=(m // tm,),
        in_specs=[_row_spec(tm, d)] * 6 + [_full_spec((1, d))] * 3
        + [_full_spec(sel.shape), _full_spec(expand.shape), _full_spec((d, d))],
        out_specs=_row_spec(tm, d),
        compiler_params=_params(("parallel",)),
        name="rwkv_out",
    )(o, r, k, v, g, x, vec(r_k), vec(ln_g), vec(ln_b), sel, expand, w_o)


def _trunk(x, seq, start_pos, conv_buf, pool_buf, kv_past, wkv0, shift0, w):
    b, t, d = x.shape
    m = b * t
    flat = lambda z: z.reshape(m, d)
    x2 = flat(x)

    u = conv_in(x2, w["norm_mix"][0], w["conv_wa"], w["conv_wb"])
    cw = (w["conv_dw"], w["conv_dw_b"], w["conv_ln_g"], w["conv_ln_b"], w["conv_w_out"])
    if seq:
        x2 = flat(conv_seq(u.reshape(b, t, d), x, *cw))
        new_conv = u.reshape(b, t, d)[:, t - (CONV_WIDTH - 1):]
    else:
        x2 = conv_step(u, conv_buf, x2, *cw)
        new_conv = jnp.concatenate([conv_buf[:, 1:], u[:, None]], axis=1)
    x2 = mlp(x2, w["norm_mlp"][0], w["mlp_up"][0], w["mlp_down"][0])

    if seq:
        y, tail = pool_seq(x2.reshape(b, t, d), w["norm_mix"][1], w["pool_w"], w["pool_scale"])
        x2 = flat(y)
        new_pool = tail[:, POOL_HALO - POOL_BUF:]
    else:
        x2, xn = pool_step(x2, pool_buf, start_pos, w["norm_mix"][1], w["pool_w"], w["pool_scale"])
        new_pool = jnp.concatenate([pool_buf[:, 1:], xn[:, None]], axis=1)
    x2 = mlp(x2, w["norm_mlp"][1], w["mlp_up"][1], w["mlp_down"][1])

    q, k, v, kb, vb = qkv_proj(x2, w["norm_mix"][2], w["attn_wq"], w["attn_wk"], w["attn_wv"])
    if seq:
        att = sb_attn_prompt(q.reshape(b, t, d), kb.reshape(b, t, d), vb.reshape(b, t, d), w["attn_sb_bias"])
        att = flat(att)
    else:
        cache_k, cache_v, page_table = kv_past
        att = sb_attn_decode(q.astype(F32), k, v, w["attn_sb_bias"], cache_k, cache_v, page_table)
    x2 = proj_res(att, x2, w["attn_w_o"])
    new_k = k.reshape(b, t, N_HEADS, HEAD_DIM)
    new_v = v.reshape(b, t, N_HEADS, HEAD_DIM)
    x2 = mlp(x2, w["norm_mlp"][2], w["mlp_up"][2], w["mlp_down"][2])

    if seq:
        prev = flat(norm_shift(x2.reshape(b, t, d), w["norm_mix"][3]))
    else:
        prev = shift0
    r, lw, k2, v2, kk, a, g, xn = rwkv_in(x2, prev, w["norm_mix"][3], w["rw"])
    if seq:
        sh = lambda z: z.reshape(b, t, d)
        o, hstate = wkv_seq(sh(r), sh(lw), sh(k2), sh(v2), sh(kk), sh(a))
        o = flat(o)
        new_wkv = jnp.swapaxes(hstate, -1, -2)
        new_shift = xn.reshape(b, t, d)[:, t - 1]
    else:
        o, new_wkv = wkv_step(r, lw, k2, v2, kk, a, wkv0)
        new_shift = xn
    x2 = rwkv_out(o, r, k2, v2, g, x2, w["rw"]["r_k"], w["rw"]["ln_g"], w["rw"]["ln_b"], w["rw"]["w_o"])
    x2 = mlp(x2, w["norm_mlp"][3], w["mlp_up"][3], w["mlp_down"][3], g_final=w["norm_final"])

    return x2.reshape(b, t, d), new_conv, new_pool, new_k, new_v, new_wkv, new_shift


def kernel(x_prompt, x_sample, cache_conv, cache_pool, cache_k, cache_v, page_table, state_wkv, state_shift,
           norm_mix, norm_mlp, norm_final, mlp_w_up, mlp_w_down,
           conv_w_in, conv_dw, conv_dw_b, conv_ln_g, conv_ln_b, conv_w_out,
           pool_w, pool_scale, attn_w_qkv, attn_w_o, attn_sb_bias,
           rw_mix, rw_w_r, rw_w_k, rw_w_v, rw_w_o, rw_w0, rw_w1, rw_w2, rw_a0, rw_a1, rw_a2,
           rw_g1, rw_g2, rw_k_k, rw_k_a, rw_r_k, rw_ln_g, rw_ln_b):
    d = D_MODEL
    bf = lambda z: z.astype(BF16)
    gl = rw_g1.shape[-1]
    glp = -(-gl // LANES) * LANES
    g1 = jnp.pad(rw_g1[0], ((0, 0), (0, glp - gl)))
    g2 = jnp.pad(rw_g2[0], ((0, glp - gl), (0, 0)))
    w = dict(
        norm_mix=norm_mix, norm_mlp=norm_mlp, norm_final=norm_final,
        mlp_up=bf(mlp_w_up), mlp_down=bf(mlp_w_down),
        conv_wa=bf(conv_w_in[0, :, :d]), conv_wb=bf(conv_w_in[0, :, d:]),
        conv_dw=conv_dw[0], conv_dw_b=conv_dw_b[0], conv_ln_g=conv_ln_g[0], conv_ln_b=conv_ln_b[0],
        conv_w_out=bf(conv_w_out[0]),
        pool_w=bf(pool_w[0]), pool_scale=pool_scale[0],
        attn_wq=bf(attn_w_qkv[0, :, :d]), attn_wk=bf(attn_w_qkv[0, :, d:2 * d]),
        attn_wv=bf(attn_w_qkv[0, :, 2 * d:]), attn_w_o=bf(attn_w_o[0]), attn_sb_bias=attn_sb_bias[0],
        rw=dict(mix=rw_mix[0], w_r=bf(rw_w_r[0]), w_k=bf(rw_w_k[0]), w_v=bf(rw_w_v[0]), w_o=bf(rw_w_o[0]),
                w0=rw_w0[0], w1=bf(rw_w1[0]), w2=bf(rw_w2[0]), a0=rw_a0[0], a1=bf(rw_a1[0]), a2=bf(rw_a2[0]),
                g1=bf(g1), g2=bf(g2), k_k=rw_k_k[0], k_a=rw_k_a[0], r_k=rw_r_k[0].reshape(-1),
                ln_g=rw_ln_g[0], ln_b=rw_ln_b[0]),
    )
    n_phys = cache_k.shape[1]
    yp, conv_p, pool_p, k_p, v_p, wkv_p, shift_p = _trunk(
        x_prompt, True, 0, None, None, None, None, None, w)
    past_len = page_table.shape[1] * PAGE_SIZE
    ys, conv_s, pool_s, k_s, v_s, wkv_s, shift_s = _trunk(
        x_sample, False, past_len, cache_conv[0], cache_pool[0],
        (cache_k[0].reshape(n_phys, PAGE_SIZE, d), cache_v[0].reshape(n_phys, PAGE_SIZE, d), page_table),
        state_wkv[0], state_shift[0], w)
    st = lambda z: z[None]
    return (yp, ys, st(conv_p), st(conv_s), st(pool_p), st(pool_s), st(k_p), st(v_p), st(k_s), st(v_s),
            st(wkv_p), st(wkv_s), st(shift_p), st(shift_s))
```

```python
import functools

import jax
import jax.numpy as jnp
from jax import lax
from jax.experimental import pallas as pl
from jax.experimental.pallas import tpu as pltpu

F32 = jnp.float32
BF16 = jnp.bfloat16

D_MODEL = 1024
D_FF = 4 * D_MODEL
HEAD_DIM = 64
N_HEADS = D_MODEL // HEAD_DIM
CONV_WIDTH = 31
POOL_WINDOWS = (2, 4, 8, 16)
POOL_GROUP = D_MODEL // len(POOL_WINDOWS)
POOL_BUF = max(POOL_WINDOWS) - 1
PAGE_SIZE = 128
RMS_EPS = 1e-6
LN_EPS = 1e-5
GN_EPS = 1e-5 * HEAD_DIM
SB_SCALE = HEAD_DIM ** -0.5

LANES = 128
SUBLANES = 8
MXU_DIM = 256
VMEM_LIMIT = 48 << 20


def _params(sem, vmem=VMEM_LIMIT):
    return pltpu.CompilerParams(dimension_semantics=sem, vmem_limit_bytes=vmem)


def _rms(x, g):
    return x * lax.rsqrt(jnp.mean(x * x, axis=-1, keepdims=True) + RMS_EPS) * g


def _dot(a, b):
    return jnp.dot(a, b, preferred_element_type=F32)


def _dot_nt(a, b):
    return lax.dot_general(a, b, (((1,), (1,)), ((), ())), preferred_element_type=F32)


def _dot_tn(a, b):
    return lax.dot_general(a, b, (((0,), (0,)), ((), ())), preferred_element_type=F32)


def _ones_where(cond):
    return jnp.where(cond, 1.0, 0.0).astype(BF16)


def _split2(x):
    hi = x.astype(BF16)
    lo = (x - hi.astype(F32)).astype(BF16)
    return hi, lo


def _dot_x2(x, w):
    hi, lo = _split2(x)
    return _dot(hi, w) + _dot(lo, w)


def _dot_x3(w, x):
    hi = x.astype(BF16)
    r1 = x - hi.astype(F32)
    mid = r1.astype(BF16)
    lo = (r1 - mid.astype(F32)).astype(BF16)
    return _dot(w, hi) + _dot(w, mid) + _dot(w, lo)


def _row_spec(tm, n):
    return pl.BlockSpec((tm, n), lambda i: (i, 0))


def _full_spec(shape):
    return pl.BlockSpec(shape, lambda *_: (0,) * len(shape))


def _row_tile(m, pref):
    return pref if m % pref == 0 else m


def _mlp_kernel(x_ref, g_ref, gf_ref, wu_ref, wd_ref, o_ref, xn_ref, *, nj, final_norm):
    j = pl.program_id(1)

    @pl.when(j == 0)
    def _():
        x = x_ref[...]
        xn_ref[...] = _rms(x, g_ref[...]).astype(BF16)
        o_ref[...] = x

    h = _dot(xn_ref[...], wu_ref[...])
    h = jnp.square(jnp.maximum(h, 0.0)).astype(BF16)
    o_ref[...] += _dot(h, wd_ref[...])

    if final_norm:
        @pl.when(j == nj - 1)
        def _():
            o_ref[...] = _rms(o_ref[...], gf_ref[...])


def mlp(x, g, w_up, w_down, g_final=None):
    m, d = x.shape
    tm = _row_tile(m, 512)
    tf = 1024
    nj = D_FF // tf
    final_norm = g_final is not None
    gf = g_final if final_norm else g
    return pl.pallas_call(
        functools.partial(_mlp_kernel, nj=nj, final_norm=final_norm),
        out_shape=jax.ShapeDtypeStruct((m, d), F32),
        grid=(m // tm, nj),
        in_specs=[
            pl.BlockSpec((tm, d), lambda i, j: (i, 0)),
            pl.BlockSpec((1, d), lambda i, j: (0, 0)),
            pl.BlockSpec((1, d), lambda i, j: (0, 0)),
            pl.BlockSpec((d, tf), lambda i, j: (0, j)),
            pl.BlockSpec((tf, d), lambda i, j: (j, 0)),
        ],
        out_specs=pl.BlockSpec((tm, d), lambda i, j: (i, 0)),
        scratch_shapes=[pltpu.VMEM((tm, d), BF16)],
        compiler_params=_params(("parallel", "arbitrary")),
        name="mlp",
    )(x, g.reshape(1, d), gf.reshape(1, d), w_up, w_down)


def _proj_res_kernel(a_ref, x_ref, w_ref, o_ref):
    o_ref[...] = x_ref[...] + _dot(a_ref[...].astype(BF16), w_ref[...])


def proj_res(a, x, w):
    m, d = x.shape
    tm = _row_tile(m, 512)
    return pl.pallas_call(
        _proj_res_kernel,
        out_shape=jax.ShapeDtypeStruct((m, d), F32),
        grid=(m // tm,),
        in_specs=[_row_spec(tm, a.shape[1]), _row_spec(tm, d), _full_spec(w.shape)],
        out_specs=_row_spec(tm, d),
        compiler_params=_params(("parallel",)),
        name="proj_res",
    )(a, x, w)


def _conv_in_kernel(x_ref, g_ref, wa_ref, wb_ref, o_ref):
    xn = _rms(x_ref[...], g_ref[...]).astype(BF16)
    o_ref[...] = _dot(xn, wa_ref[...]) * jax.nn.sigmoid(_dot(xn, wb_ref[...]))


def conv_in(x, g, w_a, w_b):
    m, d = x.shape
    tm = _row_tile(m, 512)
    return pl.pallas_call(
        _conv_in_kernel,
        out_shape=jax.ShapeDtypeStruct((m, d), F32),
        grid=(m // tm,),
        in_specs=[_row_spec(tm, d), _full_spec((1, d)), _full_spec(w_a.shape), _full_spec(w_b.shape)],
        out_specs=_row_spec(tm, d),
        compiler_params=_params(("parallel",)),
        name="conv_in",
    )(x, g.reshape(1, d), w_a, w_b)


def _ln_silu_proj(h, lng, lnb, w):
    mu = jnp.mean(h, axis=-1, keepdims=True)
    hc = h - mu
    var = jnp.mean(hc * hc, axis=-1, keepdims=True)
    hn = hc * lax.rsqrt(var + LN_EPS) * lng + lnb
    hn = hn * jax.nn.sigmoid(hn)
    return _dot(hn.astype(BF16), w)


CONV_HALO = 32
CONV_ROWS = 16


def _conv_seq_kernel(u_ref, halo_ref, x_ref, dw_ref, dwb_ref, lng_ref, lnb_ref, w_ref, o_ref,
                     up_ref, h_ref, *, tm):
    i = pl.program_id(1)
    up_ref[0, 0:CONV_HALO, :] = jnp.where(i > 0, halo_ref[0], 0.0)
    up_ref[0, CONV_HALO:, :] = u_ref[0]
    n_sh = tm + CONV_HALO - SUBLANES
    for s in range(1, SUBLANES):
        up_ref[s, 0:n_sh, :] = up_ref[0, s:s + n_sh, :]
    lead = CONV_HALO - (CONV_WIDTH - 1)
    for c in range(tm // CONV_ROWS):
        r0 = c * CONV_ROWS
        acc = jnp.zeros((CONV_ROWS, D_MODEL), F32)
        for j in range(CONV_WIDTH):
            s, a0 = (lead + j) % SUBLANES, r0 + (lead + j) // SUBLANES * SUBLANES
            acc = acc + up_ref[s, a0:a0 + CONV_ROWS, :] * dw_ref[j:j + 1, :]
        h_ref[r0:r0 + CONV_ROWS, :] = acc
    h = h_ref[...] + dwb_ref[...]
    o_ref[0] = x_ref[0] + _ln_silu_proj(h, lng_ref[...], lnb_ref[...], w_ref[...])


def conv_seq(u, x, dw, dwb, lng, lnb, w_out):
    b, s, d = u.shape
    tm = 128
    nt = s // tm
    hb = tm // CONV_HALO
    vec = lambda v: v.reshape(1, d)
    return pl.pallas_call(
        functools.partial(_conv_seq_kernel, tm=tm),
        out_shape=jax.ShapeDtypeStruct((b, s, d), F32),
        grid=(b, nt),
        in_specs=[
            pl.BlockSpec((1, tm, d), lambda bi, i: (bi, i, 0)),
            pl.BlockSpec((1, CONV_HALO, d), lambda bi, i: (bi, jnp.maximum(i * hb - 1, 0), 0)),
            pl.BlockSpec((1, tm, d), lambda bi, i: (bi, i, 0)),
            pl.BlockSpec((CONV_WIDTH, d), lambda bi, i: (0, 0)),
            pl.BlockSpec((1, d), lambda bi, i: (0, 0)),
            pl.BlockSpec((1, d), lambda bi, i: (0, 0)),
            pl.BlockSpec((1, d), lambda bi, i: (0, 0)),
            pl.BlockSpec((d, d), lambda bi, i: (0, 0)),
        ],
        out_specs=pl.BlockSpec((1, tm, d), lambda bi, i: (bi, i, 0)),
        scratch_shapes=[pltpu.VMEM((SUBLANES, tm + CONV_HALO, d), F32), pltpu.VMEM((tm, d), F32)],
        compiler_params=_params(("parallel", "arbitrary")),
        name="conv_seq",
    )(u, u, x, dw, vec(dwb), vec(lng), vec(lnb), w_out)


def _conv_step_kernel(u_ref, buf_ref, x_ref, dw_ref, dwb_ref, lng_ref, lnb_ref, w_ref, o_ref):
    nb = CONV_WIDTH - 1
    h = jnp.sum(buf_ref[...] * dw_ref[0:nb, :][None], axis=1)
    h = h + u_ref[...] * dw_ref[nb:nb + 1, :] + dwb_ref[...]
    o_ref[...] = x_ref[...] + _ln_silu_proj(h, lng_ref[...], lnb_ref[...], w_ref[...])


def conv_step(u, buf, x, dw, dwb, lng, lnb, w_out):
    b, d = u.shape
    tb = 32
    vec = lambda v: v.reshape(1, d)
    return pl.pallas_call(
        _conv_step_kernel,
        out_shape=jax.ShapeDtypeStruct((b, d), F32),
        grid=(b // tb,),
        in_specs=[
            _row_spec(tb, d),
            pl.BlockSpec((tb, CONV_WIDTH - 1, d), lambda i: (i, 0, 0)),
            _row_spec(tb, d),
            _full_spec((CONV_WIDTH, d)), _full_spec((1, d)), _full_spec((1, d)), _full_spec((1, d)),
            _full_spec((d, d)),
        ],
        out_specs=_row_spec(tb, d),
        compiler_params=_params(("parallel",)),
        name="conv_step",
    )(u, buf, x, dw, vec(dwb), vec(lng), vec(lnb), w_out)


POOL_HALO = 16


def _pool_project(d, w_ref, scale):
    ys = []
    for gi in range(len(POOL_WINDOWS)):
        dg = d[:, gi * POOL_GROUP:(gi + 1) * POOL_GROUP].astype(BF16)
        ys.append(_dot(dg, w_ref[gi]))
    return jnp.concatenate(ys, axis=-1) * scale


def _pool_seq_kernel(x_ref, halo_ref, g_ref, w_ref, sc_ref, o_ref, tail_ref, xc_ref, *, tm, nt):
    i = pl.program_id(1)
    g = g_ref[...]
    xn = _rms(x_ref[0], g)
    xc_ref[0:POOL_HALO, :] = jnp.where(i > 0, _rms(halo_ref[0], g), 0.0)
    xc_ref[POOL_HALO:, :] = xn
    pos = i * tm + lax.broadcasted_iota(jnp.int32, (tm, 1), 0)
    means = []
    for gi, w in enumerate(POOL_WINDOWS):
        c = slice(gi * POOL_GROUP, (gi + 1) * POOL_GROUP)
        win = xn[:, c]
        for k in range(1, w):
            win = win + xc_ref[POOL_HALO - k:POOL_HALO - k + tm, c]
        cnt = jnp.minimum(w, pos + 1).astype(F32)
        means.append(win / cnt)
    d = jnp.concatenate(means, axis=-1) - xn
    o_ref[0] = x_ref[0] + _pool_project(d, w_ref, sc_ref[...])

    @pl.when(i == nt - 1)
    def _():
        tail_ref[0] = xn[tm - POOL_HALO:, :]


def pool_seq(x, g, w_grp, scale):
    b, s, d = x.shape
    tm = 256
    nt = s // tm
    hb = tm // POOL_HALO
    return pl.pallas_call(
        functools.partial(_pool_seq_kernel, tm=tm, nt=nt),
        out_shape=(jax.ShapeDtypeStruct((b, s, d), F32), jax.ShapeDtypeStruct((b, POOL_HALO, d), F32)),
        grid=(b, nt),
        in_specs=[
            pl.BlockSpec((1, tm, d), lambda bi, i: (bi, i, 0)),
            pl.BlockSpec((1, POOL_HALO, d), lambda bi, i: (bi, jnp.maximum(i * hb - 1, 0), 0)),
            pl.BlockSpec((1, d), lambda bi, i: (0, 0)),
            pl.BlockSpec(w_grp.shape, lambda bi, i: (0, 0, 0)),
            pl.BlockSpec((1, d), lambda bi, i: (0, 0)),
        ],
        out_specs=(pl.BlockSpec((1, tm, d), lambda bi, i: (bi, i, 0)),
                   pl.BlockSpec((1, POOL_HALO, d), lambda bi, i: (bi, 0, 0))),
        scratch_shapes=[pltpu.VMEM((tm + POOL_HALO, d), F32)],
        compiler_params=_params(("parallel", "arbitrary")),
        name="pool_seq",
    )(x, x, g.reshape(1, d), w_grp, scale.reshape(1, d))


def _pool_step_kernel(x_ref, buf_ref, g_ref, msk_ref, icnt_ref, w_ref, sc_ref, o_ref, xn_ref):
    xn = _rms(x_ref[...], g_ref[...])
    win = xn + jnp.sum(buf_ref[...] * msk_ref[...][None], axis=1)
    d = win * icnt_ref[...] - xn
    o_ref[...] = x_ref[...] + _pool_project(d, w_ref, sc_ref[...])
    xn_ref[...] = xn


def pool_step(x, buf, start_pos, g, w_grp, scale):
    b, d = x.shape
    tb = 32
    rows = jnp.arange(POOL_BUF)[:, None]
    win_of_lane = jnp.repeat(jnp.asarray(POOL_WINDOWS), POOL_GROUP)[None, :]
    msk = (rows >= POOL_BUF - (win_of_lane - 1)).astype(F32)
    icnt = 1.0 / jnp.minimum(win_of_lane, start_pos + 1).astype(F32)
    return pl.pallas_call(
        _pool_step_kernel,
        out_shape=(jax.ShapeDtypeStruct((b, d), F32), jax.ShapeDtypeStruct((b, d), F32)),
        grid=(b // tb,),
        in_specs=[
            _row_spec(tb, d),
            pl.BlockSpec((tb, POOL_BUF, d), lambda i: (i, 0, 0)),
            _full_spec((1, d)), _full_spec((POOL_BUF, d)), _full_spec((1, d)),
            _full_spec(w_grp.shape), _full_spec((1, d)),
        ],
        out_specs=(_row_spec(tb, d), _row_spec(tb, d)),
        compiler_params=_params(("parallel",)),
        name="pool_step",
    )(x, buf, g.reshape(1, d), msk, icnt, w_grp, scale.reshape(1, d))


def _qkv_kernel(x_ref, g_ref, wq_ref, wk_ref, wv_ref, q_ref, k_ref, v_ref, kb_ref, vb_ref):
    xn = _rms(x_ref[...], g_ref[...]).astype(BF16)
    q_ref[...] = (_dot(xn, wq_ref[...]) * (SB_SCALE * LOG2E)).astype(BF16)
    k = _dot(xn, wk_ref[...])
    v = _dot(xn, wv_ref[...])
    k_ref[...] = k
    v_ref[...] = v
    kb_ref[...] = k.astype(BF16)
    vb_ref[...] = v.astype(BF16)


def qkv_proj(x, g, wq, wk, wv):
    m, d = x.shape
    tm = _row_tile(m, 512)
    sd = lambda dt: jax.ShapeDtypeStruct((m, d), dt)
    return pl.pallas_call(
        _qkv_kernel,
        out_shape=(sd(BF16), sd(F32), sd(F32), sd(BF16), sd(BF16)),
        grid=(m // tm,),
        in_specs=[_row_spec(tm, d), _full_spec((1, d)), _full_spec((d, d)), _full_spec((d, d)),
                  _full_spec((d, d))],
        out_specs=tuple(_row_spec(tm, d) for _ in range(5)),
        compiler_params=_params(("parallel",)),
        name="qkv_proj",
    )(x, g.reshape(1, d), wq, wk, wv)


def _softplus(z):
    return jnp.maximum(z, 0.0) + jnp.log(1.0 + jnp.exp(-jnp.abs(z)))


ATT_BLK = 256


LOG2E = 1.4426950408889634
ATT_ROWS = 128


def _softplus2(z):
    neg_abs = lax.bitcast_convert_type(
        lax.bitcast_convert_type(z, jnp.int32) | jnp.int32(-2 ** 31), F32)
    return jnp.maximum(z, 0.0) + jnp.log2(1.0 + jnp.exp2(neg_abs))


def _sb_prompt_kernel(bias_ref, q_ref, k_ref, v_ref, o_ref, z_scr, p_scr):
    hp = pl.program_id(1)
    i = pl.program_id(2)
    t = ATT_BLK
    nr = t // ATT_ROWS
    row = lax.broadcasted_iota(jnp.int32, (t, t), 0)
    col = lax.broadcasted_iota(jnp.int32, (t, t), 1)
    neg_upper = jnp.where(row > col, -1.0, 0.0).astype(BF16)
    diag_mask = col < row
    heads = [slice(hh * HEAD_DIM, (hh + 1) * HEAD_DIM) for hh in range(2)]
    rows = [slice(r * ATT_ROWS, (r + 1) * ATT_ROWS) for r in range(nr)]
    chains = [(hh, r) for hh in range(2) for r in range(nr)]
    nc = len(chains)
    bias = [bias_ref[2 * hp + hh] * LOG2E for hh in range(2)]
    q = [q_ref[0, rows[r], heads[hh]] for hh, r in chains]

    def logits(blk):
        kb = k_ref[0, pl.ds(pl.multiple_of(blk * t, t), t), :]
        return [_dot_nt(q[ci], kb[:, heads[hh]]) + bias[hh] for ci, (hh, r) in enumerate(chains)]

    def stash_logits(zs):
        for ci in range(nc):
            z_scr[ci] = zs[ci]

    def stash_weights(zs, cs, mask):
        sps, zsp = [], []
        for z, (hh, r) in zip(zs, chains):
            sp = _softplus2(z)
            sps.append(sp if mask is None else jnp.where(mask[rows[r], :], sp, 0.0))
            zsp.append(z - sp)
        betweens = [_dot(sp.astype(BF16), neg_upper) for sp in sps]
        cs_new = []
        for ci, (x, bt, sp, c, (hh, r)) in enumerate(zip(zsp, betweens, sps, cs, chains)):
            p = jnp.exp2(x + bt + c)
            p_scr[ci] = (p if mask is None else jnp.where(mask[rows[r], :], p, 0.0)).astype(BF16)
            cs_new.append(c + bt[:, 0:1] - sp[:, 0:1])
        return cs_new

    def weighted_values(blk, accs):
        vb = v_ref[0, pl.ds(pl.multiple_of(blk * t, t), t), :]
        return [acc + _dot(p_scr[ci], vb[:, heads[hh]]) for ci, (acc, (hh, r)) in enumerate(zip(accs, chains))]

    z_diag = logits(i)
    stash_logits(logits(jnp.maximum(i - 1, 0)))
    cs = stash_weights(z_diag, [jnp.zeros((ATT_ROWS, 1), F32)] * nc, diag_mask)

    def body(n, carry):
        cs, accs = carry
        blk = i - 1 - n
        zs = [z_scr[ci] for ci in range(nc)]
        stash_logits(logits(jnp.maximum(blk - 1, 0)))
        accs = weighted_values(blk + 1, accs)
        cs = stash_weights(zs, cs, None)
        return cs, accs

    accs = [jnp.zeros((ATT_ROWS, HEAD_DIM), F32)] * nc
    cs, accs = lax.fori_loop(0, i, body, (cs, accs))
    accs = weighted_values(0, accs)
    o_ref[0] = jnp.concatenate(
        [jnp.concatenate([accs[hh * nr + r] for r in range(nr)], axis=0) for hh in range(2)], axis=-1)


def sb_attn_prompt(q, k, v, bias):
    b, s, d = q.shape
    t = ATT_BLK
    return pl.pallas_call(
        _sb_prompt_kernel,
        out_shape=jax.ShapeDtypeStruct((b, s, d), F32),
        grid_spec=pltpu.PrefetchScalarGridSpec(
            num_scalar_prefetch=0,
            grid=(b, N_HEADS // 2, s // t),
            in_specs=[
                pl.BlockSpec(memory_space=pltpu.SMEM),
                pl.BlockSpec((1, t, LANES), lambda bi, hp, i: (bi, i, hp)),
                pl.BlockSpec((1, s, LANES), lambda bi, hp, i: (bi, 0, hp)),
                pl.BlockSpec((1, s, LANES), lambda bi, hp, i: (bi, 0, hp)),
            ],
            out_specs=pl.BlockSpec((1, t, LANES), lambda bi, hp, i: (bi, i, hp)),
            scratch_shapes=[pltpu.VMEM((2 * t // ATT_ROWS, ATT_ROWS, t), F32),
                            pltpu.VMEM((2 * t // ATT_ROWS, ATT_ROWS, t), BF16)],
        ),
        compiler_params=_params(("parallel", "parallel", "arbitrary")),
        name="sb_attn_prompt",
    )(bias, q, k, v)


DEC_PAGES = 4


def _sb_decode_kernel(pt_ref, q_ref, kn_ref, vn_ref, bias_ref, *refs, n_steps):
    del pt_ref
    kp_refs, vp_refs = refs[:DEC_PAGES], refs[DEC_PAGES:2 * DEC_PAGES]
    o_ref, c_ref, acc_ref = refs[2 * DEC_PAGES:]
    p = pl.program_id(1)
    t = PAGE_SIZE
    q = q_ref[...]
    bias = bias_ref[...] * LOG2E

    @pl.when(p == 0)
    def _():
        zn = jnp.sum(kn_ref[0] * q[0], axis=-1, keepdims=True) + bias
        q_pos = n_steps * DEC_PAGES * PAGE_SIZE
        visible = (q_pos + lax.broadcasted_iota(jnp.int32, zn.shape, 1)) < q_pos
        spn = _softplus2(zn)
        c_ref[...] = jnp.where(visible, -spn, 0.0)
        acc_ref[...] = jnp.where(visible, jnp.exp2(zn - spn), 0.0) * vn_ref[0]

    tok = lax.broadcasted_iota(jnp.int32, (t, 1, t), 0)
    lane = lax.broadcasted_iota(jnp.int32, (t, 1, t), 2)
    onehot = tok == lane
    row = lax.broadcasted_iota(jnp.int32, (t, t), 0)
    col = lax.broadcasted_iota(jnp.int32, (t, t), 1)
    upper = _ones_where(row > col)
    c = c_ref[...]
    acc = acc_ref[...]
    for g in range(DEC_PAGES):
        z3 = jnp.sum(kp_refs[g][0] * q, axis=-1, keepdims=True)
        z = jnp.sum(jnp.where(onehot, z3, 0.0), axis=0) + bias
        sp = _softplus2(z)
        lk = -sp
        between = _dot(lk.astype(BF16), upper)
        a = jnp.exp2(z - sp + between + c)
        a3 = jnp.sum(jnp.where(onehot, a[None], 0.0), axis=-1, keepdims=True)
        acc = acc + jnp.sum(a3 * vp_refs[g][0], axis=0)
        c = c + jnp.sum(lk, axis=-1, keepdims=True)
    c_ref[...] = c
    acc_ref[...] = acc

    @pl.when(p == n_steps - 1)
    def _():
        o_ref[0] = acc


def sb_attn_decode(q, k_new, v_new, bias, cache_k, cache_v, page_table):
    b = q.shape[0]
    n_pages = page_table.shape[1]
    n_steps = n_pages // DEC_PAGES
    hd = (N_HEADS, HEAD_DIM)

    def page(g):
        return lambda bi, p, pt: (pt[bi * n_pages + (n_pages - 1 - (p * DEC_PAGES + g))], 0, 0, 0)

    row_spec = pl.BlockSpec((1,) + hd, lambda bi, p, pt: (bi, 0, 0))
    page_specs = [pl.BlockSpec((1, PAGE_SIZE) + hd, page(g)) for g in range(DEC_PAGES)]
    return pl.pallas_call(
        functools.partial(_sb_decode_kernel, n_steps=n_steps),
        out_shape=jax.ShapeDtypeStruct((b,) + hd, F32),
        grid_spec=pltpu.PrefetchScalarGridSpec(
            num_scalar_prefetch=1,
            grid=(b, n_steps),
            in_specs=[row_spec, row_spec, row_spec, pl.BlockSpec((N_HEADS, 1), lambda bi, p, pt: (0, 0))]
            + page_specs + page_specs,
            out_specs=row_spec,
            scratch_shapes=[pltpu.VMEM((N_HEADS, 1), F32), pltpu.VMEM(hd, F32)],
        ),
        compiler_params=_params(("parallel", "arbitrary")),
        name="sb_attn_decode",
    )(page_table.reshape(-1), q, k_new, v_new, bias.reshape(N_HEADS, 1),
      *([cache_k] * DEC_PAGES), *([cache_v] * DEC_PAGES))


def _head_sum(x, sel, expand):
    return _dot_x2(_dot_x2(x, sel), expand)


def _rwkv_in_kernel(x_ref, prev_ref, g_ref, mix_ref, wr_ref, wk_ref, wv_ref, w0_ref, w1_ref, w2_ref,
                    a0_ref, a1_ref, a2_ref, g1_ref, g2_ref, kk_ref, ka_ref, sel_ref, exp_ref,
                    r_o, lw_o, k_o, v_o, kk_o, a_o, g_o, xn_o):
    xn = _rms(x_ref[...], g_ref[...])
    xx = prev_ref[...] - xn
    mixed = lambda c: (xn + xx * mix_ref[c:c + 1, :]).astype(BF16)
    xr, xw, xk, xv, xa, xg = (mixed(c) for c in range(6))
    r = _dot(xr, wr_ref[...])
    w = w0_ref[...] + _dot(jnp.tanh(_dot(xw, w1_ref[...])).astype(BF16), w2_ref[...])
    w = -_softplus(-w) - 0.5
    k = _dot(xk, wk_ref[...])
    v = _dot(xv, wv_ref[...])
    a = jax.nn.sigmoid(a0_ref[...] + _dot(_dot(xa, a1_ref[...]).astype(BF16), a2_ref[...]))
    g = _dot(jax.nn.sigmoid(_dot(xg, g1_ref[...])).astype(BF16), g2_ref[...])
    kk = k * kk_ref[...]
    nrm = jnp.sqrt(_head_sum(kk * kk, sel_ref[...], exp_ref[...]))
    kk = kk / jnp.maximum(nrm, 1e-12)
    r_o[...] = r
    lw_o[...] = -jnp.exp(w)
    k_o[...] = k * (1.0 + (a - 1.0) * ka_ref[...])
    v_o[...] = v
    kk_o[...] = kk
    a_o[...] = a
    g_o[...] = g
    xn_o[...] = xn


def _head_selectors():
    lane = jnp.arange(D_MODEL)[:, None] // HEAD_DIM
    sel = (lane == jnp.arange(LANES)[None, :]).astype(BF16)
    return sel, sel.T


def rwkv_in(x, x_prev_rows, g, p):
    m, d = x.shape
    tm = _row_tile(m, 256)
    sel, expand = _head_selectors()
    vec = lambda v: v.reshape(1, d)
    ws = [p["mix"], p["w_r"], p["w_k"], p["w_v"], vec(p["w0"]), p["w1"], p["w2"], vec(p["a0"]), p["a1"],
          p["a2"], p["g1"], p["g2"], vec(p["k_k"]), vec(p["k_a"]), sel, expand]
    sd = jax.ShapeDtypeStruct((m, d), F32)
    return pl.pallas_call(
        _rwkv_in_kernel,
        out_shape=(sd,) * 8,
        grid=(m // tm,),
        in_specs=[_row_spec(tm, d), _row_spec(tm, d), _full_spec((1, d))] + [_full_spec(w.shape) for w in ws],
        out_specs=tuple(_row_spec(tm, d) for _ in range(8)),
        compiler_params=_params(("parallel",)),
        name="rwkv_in",
    )(x, x_prev_rows, vec(g), *ws)


def _norm_shift_kernel(x_ref, halo_ref, g_ref, o_ref, *, tm):
    i = pl.program_id(1)
    g = g_ref[...]
    xn = _rms(x_ref[0], g)
    last = _rms(halo_ref[0], g)[7:8, :]
    first = jnp.where(i > 0, last, 0.0)
    rolled = pltpu.roll(xn, 1, axis=0)
    row = lax.broadcasted_iota(jnp.int32, (tm, 1), 0)
    o_ref[0] = jnp.where(row == 0, first, rolled)


def norm_shift(x, g):
    b, s, d = x.shape
    tm = 512
    hb = tm // 8
    return pl.pallas_call(
        functools.partial(_norm_shift_kernel, tm=tm),
        out_shape=jax.ShapeDtypeStruct((b, s, d), F32),
        grid=(b, s // tm),
        in_specs=[
            pl.BlockSpec((1, tm, d), lambda bi, i: (bi, i, 0)),
            pl.BlockSpec((1, 8, d), lambda bi, i: (bi, jnp.maximum(i * hb - 1, 0), 0)),
            pl.BlockSpec((1, d), lambda bi, i: (0, 0)),
        ],
        out_specs=pl.BlockSpec((1, tm, d), lambda bi, i: (bi, i, 0)),
        compiler_params=_params(("parallel", "arbitrary")),
        name="norm_shift",
    )(x, x, g.reshape(1, d))


WKV_CHUNK = 64


def _wkv_seq_kernel(r_ref, lw_ref, k_ref, v_ref, kk_ref, a_ref, o_ref, s_ref, h_ref, *, tb, nt):
    i = pl.program_id(1)
    c = WKV_CHUNK
    dh = HEAD_DIM

    @pl.when(i == 0)
    def _():
        h_ref[...] = jnp.zeros_like(h_ref)

    row = lax.broadcasted_iota(jnp.int32, (c, c), 0)
    col = lax.broadcasted_iota(jnp.int32, (c, c), 1)
    tri_incl = _ones_where(col <= row)
    eye = row == col
    row2 = lax.broadcasted_iota(jnp.int32, (2 * c, 2 * c), 0)
    col2 = lax.broadcasted_iota(jnp.int32, (2 * c, 2 * c), 1)
    colm = jnp.where(col2 < c, col2, col2 - c)
    gmask = colm < jnp.where(row2 < c, row2, row2 - c + 1)

    def chunk(n, carry):
        t0 = pl.multiple_of(n * c, c)
        sl = pl.ds(t0, c)
        lw = lw_ref[0, sl, :]
        cl = _dot_x3(tri_incl, lw)
        kk = kk_ref[0, sl, :]
        cl_end = cl[c - 1:c, :]
        g_end = jnp.exp(cl_end)
        inv = jnp.exp(-cl)
        at = (-kk * jnp.exp(cl - lw)).astype(BF16)
        rt_f = r_ref[0, sl, :] * jnp.exp(cl)
        rt = rt_f.astype(BF16)
        b = kk * a_ref[0, sl, :]
        k = k_ref[0, sl, :]
        bt = (b * inv).astype(BF16)
        kt = (k * inv).astype(BF16)
        to_end = jnp.exp(cl_end - cl)
        be = (b * to_end).astype(BF16)
        ke = (k * to_end).astype(BF16)
        vb = v_ref[0, sl, :].astype(BF16)
        heads = [slice(h * dh, (h + 1) * dh) for h in range(N_HEADS)]
        gms = [jnp.where(gmask, _dot_nt(jnp.concatenate([at[:, ls], rt[:, ls]], axis=0),
                                        jnp.concatenate([bt[:, ls], kt[:, ls]], axis=0)), 0.0)
               for ls in heads]
        mabs = [gm[0:c, 0:c] for gm in gms]
        mvs = [_dot(gm[0:c, c:].astype(BF16), vb[:, ls]) for gm, ls in zip(gms, heads)]
        pws = [m.astype(BF16) for m in mabs]
        tms = [jnp.where(eye, 1.0, m) for m in mabs]
        for _ in range(5):
            pws = [_dot(pw, pw).astype(BF16) for pw in pws]
            tms = [tm_ + _dot(tm_.astype(BF16), pw) for tm_, pw in zip(tms, pws)]
        aws = [_dot(tm_.astype(BF16), jnp.concatenate([at[:, ls], mv.astype(BF16)], axis=1))
               for tm_, mv, ls in zip(tms, mvs, heads)]
        zmats = [jnp.concatenate([aw.astype(BF16),
                                  jnp.concatenate([jnp.zeros((c, dh), BF16), vb[:, ls]], axis=1)], axis=0)
                 for aw, ls in zip(aws, heads)]
        x1s = [_dot(gm[c:, :].astype(BF16), zm) for gm, zm in zip(gms, zmats)]
        x2s = [_dot_tn(jnp.concatenate([be[:, ls], ke[:, ls]], axis=0), zm)
               for zm, ls in zip(zmats, heads)]
        ress = []
        for h, (x1, x2, ls) in enumerate(zip(x1s, x2s, heads)):
            dmat = jnp.where(eye, jnp.broadcast_to(g_end[:, ls], (c, dh)), 0.0)
            lb = jnp.concatenate([rt_f[:, ls] + x1[:, 0:dh], dmat + x2[:, 0:dh]], axis=0).astype(BF16)
            hi, lo = _split2(h_ref[h])
            ress.append(_dot(lb, hi) + _dot(lb, lo))
        for h, (res, x2) in enumerate(zip(ress, x2s)):
            h_ref[h] = res[c:, :] + x2[:, dh:]
        o_ref[0, sl, :] = jnp.concatenate([res[0:c, :] + x1[:, dh:] for res, x1 in zip(ress, x1s)], axis=-1)
        return carry

    lax.fori_loop(0, tb // c, chunk, 0)

    @pl.when(i == nt - 1)
    def _():
        s_ref[0] = h_ref[...]


def wkv_seq(r, lw, k, v, kk, a):
    b, s, d = r.shape
    tb = 256
    nt = s // tb
    blk = pl.BlockSpec((1, tb, d), lambda bi, i: (bi, i, 0))
    return pl.pallas_call(
        functools.partial(_wkv_seq_kernel, tb=tb, nt=nt),
        out_shape=(jax.ShapeDtypeStruct((b, s, d), F32),
                   jax.ShapeDtypeStruct((b, N_HEADS, HEAD_DIM, HEAD_DIM), F32)),
        grid=(b, nt),
        in_specs=[blk] * 6,
        out_specs=(blk, pl.BlockSpec((1, N_HEADS, HEAD_DIM, HEAD_DIM), lambda bi, i: (bi, 0, 0, 0))),
        scratch_shapes=[pltpu.VMEM((N_HEADS, HEAD_DIM, HEAD_DIM), F32)],
        compiler_params=_params(("parallel", "arbitrary")),
        name="wkv_seq",
    )(r, lw, k, v, kk, a)


def _wkv_step_kernel(r_ref, lw_ref, k_ref, v_ref, kk_ref, a_ref, s0_ref, o_ref, s_ref, *, tb):
    dh = HEAD_DIM
    eye = (lax.broadcasted_iota(jnp.int32, (1, dh, dh), 1) == lax.broadcasted_iota(jnp.int32, (1, dh, dh), 2))
    bs = range(tb)
    along_keys = lambda ref, bi: ref[bi][:, None, :]
    s0 = [s0_ref[bi] for bi in bs]
    kk = [along_keys(kk_ref, bi) for bi in bs]
    v_col = [jnp.sum(jnp.where(eye, along_keys(v_ref, bi), 0.0), axis=-1, keepdims=True) for bi in bs]
    sa = [jnp.sum(s0[bi] * -kk[bi], axis=-1, keepdims=True) for bi in bs]
    s1 = [s0[bi] * jnp.exp(along_keys(lw_ref, bi)) + sa[bi] * (kk[bi] * along_keys(a_ref, bi))
          + v_col[bi] * along_keys(k_ref, bi) for bi in bs]
    for bi in bs:
        s_ref[bi] = s1[bi]
    o_col = [jnp.sum(s1[bi] * along_keys(r_ref, bi), axis=-1, keepdims=True) for bi in bs]
    for bi in bs:
        o_ref[bi] = jnp.sum(jnp.where(eye, o_col[bi], 0.0), axis=1)


def wkv_step(r, lw, k, v, kk, a, s0):
    b = r.shape[0]
    tb = 8
    hd = (N_HEADS, HEAD_DIM)
    rows = pl.BlockSpec((tb,) + hd, lambda i: (i, 0, 0))
    st = pl.BlockSpec((tb,) + hd + (HEAD_DIM,), lambda i: (i, 0, 0, 0))
    return pl.pallas_call(
        functools.partial(_wkv_step_kernel, tb=tb),
        out_shape=(jax.ShapeDtypeStruct((b,) + hd, F32), jax.ShapeDtypeStruct(s0.shape, F32)),
        grid=(b // tb,),
        in_specs=[rows] * 6 + [st],
        out_specs=(rows, st),
        compiler_params=_params(("parallel",)),
        name="wkv_step",
    )(r, lw, k, v, kk, a, s0)


def _rwkv_out_kernel(o_ref, r_ref, k_ref, v_ref, g_ref, x_ref, rk_ref, lng_ref, lnb_ref, sel_ref, exp_ref,
                     w_ref, y_ref):
    sel, expand = sel_ref[...], exp_ref[...]
    o = o_ref[...]
    mu = _head_sum(o, sel, expand) * (1.0 / HEAD_DIM)
    oc = o - mu
    var = _head_sum(oc * oc, sel, expand) * (1.0 / HEAD_DIM)
    on = oc * lax.rsqrt(var + GN_EPS) * lng_ref[...] + lnb_ref[...]
    bonus = _head_sum(r_ref[...] * k_ref[...] * rk_ref[...], sel, expand)
    on = on + bonus * v_ref[...]
    y_ref[...] = x_ref[...] + _dot((on * g_ref[...]).astype(BF16), w_ref[...])


def rwkv_out(o, r, k, v, g, x, r_k, ln_g, ln_b, w_o):
    m, d = x.shape
    tm = _row_tile(m, 256)
    sel, expand = _head_selectors()
    vec = lambda t: t.reshape(1, d)
    return pl.pallas_call(
        _rwkv_out_kernel,
        out_shape=jax.ShapeDtypeStruct((m, d), F32),
        grid=(m // tm,),
        in_specs=[_row_spec(tm, d)] * 6 + [_full_spec((1, d))] * 3
        + [_full_spec(sel.shape), _full_spec(expand.shape), _full_spec((d, d))],
        out_specs=_row_spec(tm, d),
        compiler_params=_params(("parallel",)),
        name="rwkv_out",
    )(o, r, k, v, g, x, vec(r_k), vec(ln_g), vec(ln_b), sel, expand, w_o)


def _trunk(x, seq, start_pos, conv_buf, pool_buf, kv_past, wkv0, shift0, w):
    b, t, d = x.shape
    m = b * t
    flat = lambda z: z.reshape(m, d)
    x2 = flat(x)

    u = conv_in(x2, w["norm_mix"][0], w["conv_wa"], w["conv_wb"])
    cw = (w["conv_dw"], w["conv_dw_b"], w["conv_ln_g"], w["conv_ln_b"], w["conv_w_out"])
    if seq:
        x2 = flat(conv_seq(u.reshape(b, t, d), x, *cw))
        new_conv = u.reshape(b, t, d)[:, t - (CONV_WIDTH - 1):]
    else:
        x2 = conv_step(u, conv_buf, x2, *cw)
        new_conv = jnp.concatenate([conv_buf[:, 1:], u[:, None]], axis=1)
    x2 = mlp(x2, w["norm_mlp"][0], w["mlp_up"][0], w["mlp_down"][0])

    if seq:
        y, tail = pool_seq(x2.reshape(b, t, d), w["norm_mix"][1], w["pool_w"], w["pool_scale"])
        x2 = flat(y)
        new_pool = tail[:, POOL_HALO - POOL_BUF:]
    else:
        x2, xn = pool_step(x2, pool_buf, start_pos, w["norm_mix"][1], w["pool_w"], w["pool_scale"])
        new_pool = jnp.concatenate([pool_buf[:, 1:], xn[:, None]], axis=1)
    x2 = mlp(x2, w["norm_mlp"][1], w["mlp_up"][1], w["mlp_down"][1])

    q, k, v, kb, vb = qkv_proj(x2, w["norm_mix"][2], w["attn_wq"], w["attn_wk"], w["attn_wv"])
    if seq:
        att = sb_attn_prompt(q.reshape(b, t, d), kb.reshape(b, t, d), vb.reshape(b, t, d), w["attn_sb_bias"])
        att = flat(att)
    else:
        cache_k, cache_v, page_table = kv_past
        hd = lambda z: z.astype(F32).reshape(b, N_HEADS, HEAD_DIM)
        att = sb_attn_decode(hd(q), hd(k), hd(v), w["attn_sb_bias"], cache_k, cache_v, page_table)
        att = att.reshape(b, d)
    x2 = proj_res(att, x2, w["attn_w_o"])
    new_k = k.reshape(b, t, N_HEADS, HEAD_DIM)
    new_v = v.reshape(b, t, N_HEADS, HEAD_DIM)
    x2 = mlp(x2, w["norm_mlp"][2], w["mlp_up"][2], w["mlp_down"][2])

    if seq:
        prev = flat(norm_shift(x2.reshape(b, t, d), w["norm_mix"][3]))
    else:
        prev = shift0
    r, lw, k2, v2, kk, a, g, xn = rwkv_in(x2, prev, w["norm_mix"][3], w["rw"])
    if seq:
        sh = lambda z: z.reshape(b, t, d)
        o, hstate = wkv_seq(sh(r), sh(lw), sh(k2), sh(v2), sh(kk), sh(a))
        o = flat(o)
        new_wkv = jnp.swapaxes(hstate, -1, -2)
        new_shift = xn.reshape(b, t, d)[:, t - 1]
    else:
        hd = lambda z: z.reshape(b, N_HEADS, HEAD_DIM)
        o, new_wkv = wkv_step(hd(r), hd(lw), hd(k2), hd(v2), hd(kk), hd(a), wkv0)
        o = o.reshape(b, d)
        new_shift = xn
    x2 = rwkv_out(o, r, k2, v2, g, x2, w["rw"]["r_k"], w["rw"]["ln_g"], w["rw"]["ln_b"], w["rw"]["w_o"])
    x2 = mlp(x2, w["norm_mlp"][3], w["mlp_up"][3], w["mlp_down"][3], g_final=w["norm_final"])

    return x2.reshape(b, t, d), new_conv, new_pool, new_k, new_v, new_wkv, new_shift


def kernel(x_prompt, x_sample, cache_conv, cache_pool, cache_k, cache_v, page_table, state_wkv, state_shift,
           norm_mix, norm_mlp, norm_final, mlp_w_up, mlp_w_down,
           conv_w_in, conv_dw, conv_dw_b, conv_ln_g, conv_ln_b, conv_w_out,
           pool_w, pool_scale, attn_w_qkv, attn_w_o, attn_sb_bias,
           rw_mix, rw_w_r, rw_w_k, rw_w_v, rw_w_o, rw_w0, rw_w1, rw_w2, rw_a0, rw_a1, rw_a2,
           rw_g1, rw_g2, rw_k_k, rw_k_a, rw_r_k, rw_ln_g, rw_ln_b):
    d = D_MODEL
    bf = lambda z: z.astype(BF16)
    gl = rw_g1.shape[-1]
    glp = -(-gl // LANES) * LANES
    g1 = jnp.pad(rw_g1[0], ((0, 0), (0, glp - gl)))
    g2 = jnp.pad(rw_g2[0], ((0, glp - gl), (0, 0)))
    w = dict(
        norm_mix=norm_mix, norm_mlp=norm_mlp, norm_final=norm_final,
        mlp_up=bf(mlp_w_up), mlp_down=bf(mlp_w_down),
        conv_wa=bf(conv_w_in[0, :, :d]), conv_wb=bf(conv_w_in[0, :, d:]),
        conv_dw=conv_dw[0], conv_dw_b=conv_dw_b[0], conv_ln_g=conv_ln_g[0], conv_ln_b=conv_ln_b[0],
        conv_w_out=bf(conv_w_out[0]),
        pool_w=bf(pool_w[0]), pool_scale=pool_scale[0],
        attn_wq=bf(attn_w_qkv[0, :, :d]), attn_wk=bf(attn_w_qkv[0, :, d:2 * d]),
        attn_wv=bf(attn_w_qkv[0, :, 2 * d:]), attn_w_o=bf(attn_w_o[0]), attn_sb_bias=attn_sb_bias[0],
        rw=dict(mix=rw_mix[0], w_r=bf(rw_w_r[0]), w_k=bf(rw_w_k[0]), w_v=bf(rw_w_v[0]), w_o=bf(rw_w_o[0]),
                w0=rw_w0[0], w1=bf(rw_w1[0]), w2=bf(rw_w2[0]), a0=rw_a0[0], a1=bf(rw_a1[0]), a2=bf(rw_a2[0]),
                g1=bf(g1), g2=bf(g2), k_k=rw_k_k[0], k_a=rw_k_a[0], r_k=rw_r_k[0].reshape(-1),
                ln_g=rw_ln_g[0], ln_b=rw_ln_b[0]),
    )
    yp, conv_p, pool_p, k_p, v_p, wkv_p, shift_p = _trunk(
        x_prompt, True, 0, None, None, None, None, None, w)
    past_len = page_table.shape[1] * PAGE_SIZE
    ys, conv_s, pool_s, k_s, v_s, wkv_s, shift_s = _trunk(
        x_sample, False, past_len, cache_conv[0], cache_pool[0],
        (cache_k[0], cache_v[0], page_table),
        state_wkv[0], state_shift[0], w)
    st = lambda z: z[None]
    return (yp, ys, st(conv_p), st(conv_s), st(pool_p), st(pool_s), st(k_p), st(v_p), st(k_s), st(v_s),
            st(wkv_p), st(wkv_s), st(shift_p), st(shift_s))
```

```python
import functools

import jax
import jax.numpy as jnp
from jax import lax
from jax.experimental import pallas as pl
from jax.experimental.pallas import tpu as pltpu

F32 = jnp.float32
BF16 = jnp.bfloat16

D_MODEL = 1024
D_FF = 4 * D_MODEL
HEAD_DIM = 64
N_HEADS = D_MODEL // HEAD_DIM
CONV_WIDTH = 31
POOL_WINDOWS = (2, 4, 8, 16)
POOL_GROUP = D_MODEL // len(POOL_WINDOWS)
POOL_BUF = max(POOL_WINDOWS) - 1
PAGE_SIZE = 128
RMS_EPS = 1e-6
LN_EPS = 1e-5
GN_EPS = 1e-5 * HEAD_DIM
SB_SCALE = HEAD_DIM ** -0.5

LANES = 128
SUBLANES = 8
MXU_DIM = 256
VMEM_LIMIT = 48 << 20


def _params(sem, vmem=VMEM_LIMIT):
    return pltpu.CompilerParams(dimension_semantics=sem, vmem_limit_bytes=vmem)


def _rms(x, g):
    return x * lax.rsqrt(jnp.mean(x * x, axis=-1, keepdims=True) + RMS_EPS) * g


def _dot(a, b):
    return jnp.dot(a, b, preferred_element_type=F32)


def _dot_nt(a, b):
    return lax.dot_general(a, b, (((1,), (1,)), ((), ())), preferred_element_type=F32)


def _dot_tn(a, b):
    return lax.dot_general(a, b, (((0,), (0,)), ((), ())), preferred_element_type=F32)


def _ones_where(cond):
    return jnp.where(cond, 1.0, 0.0).astype(BF16)


def _split2(x):
    hi = x.astype(BF16)
    lo = (x - hi.astype(F32)).astype(BF16)
    return hi, lo


def _dot_x2(x, w):
    hi, lo = _split2(x)
    return _dot(hi, w) + _dot(lo, w)


def _dot_x3(w, x):
    hi = x.astype(BF16)
    r1 = x - hi.astype(F32)
    mid = r1.astype(BF16)
    lo = (r1 - mid.astype(F32)).astype(BF16)
    return _dot(w, hi) + _dot(w, mid) + _dot(w, lo)


def _row_spec(tm, n):
    return pl.BlockSpec((tm, n), lambda i: (i, 0))


def _full_spec(shape):
    return pl.BlockSpec(shape, lambda *_: (0,) * len(shape))


def _row_tile(m, pref):
    return pref if m % pref == 0 else m


def _mlp_kernel(x_ref, g_ref, gf_ref, wu_ref, wd_ref, o_ref, xn_ref, *, nj, final_norm):
    j = pl.program_id(1)

    @pl.when(j == 0)
    def _():
        x = x_ref[...]
        xn_ref[...] = _rms(x, g_ref[...]).astype(BF16)
        o_ref[...] = x

    h = _dot(xn_ref[...], wu_ref[...])
    h = jnp.square(jnp.maximum(h, 0.0)).astype(BF16)
    o_ref[...] += _dot(h, wd_ref[...])

    if final_norm:
        @pl.when(j == nj - 1)
        def _():
            o_ref[...] = _rms(o_ref[...], gf_ref[...])


def mlp(x, g, w_up, w_down, g_final=None):
    m, d = x.shape
    tm = _row_tile(m, 1024)
    tf = 1024
    nj = D_FF // tf
    final_norm = g_final is not None
    gf = g_final if final_norm else g
    return pl.pallas_call(
        functools.partial(_mlp_kernel, nj=nj, final_norm=final_norm),
        out_shape=jax.ShapeDtypeStruct((m, d), F32),
        grid=(m // tm, nj),
        in_specs=[
            pl.BlockSpec((tm, d), lambda i, j: (i, 0)),
            pl.BlockSpec((1, d), lambda i, j: (0, 0)),
            pl.BlockSpec((1, d), lambda i, j: (0, 0)),
            pl.BlockSpec((d, tf), lambda i, j: (0, j)),
            pl.BlockSpec((tf, d), lambda i, j: (j, 0)),
        ],
        out_specs=pl.BlockSpec((tm, d), lambda i, j: (i, 0)),
        scratch_shapes=[pltpu.VMEM((tm, d), BF16)],
        compiler_params=_params(("parallel", "arbitrary")),
        name="mlp",
    )(x, g.reshape(1, d), gf.reshape(1, d), w_up, w_down)


def _proj_res_kernel(a_ref, x_ref, w_ref, o_ref):
    o_ref[...] = x_ref[...] + _dot(a_ref[...].astype(BF16), w_ref[...])


def proj_res(a, x, w):
    m, d = x.shape
    tm = _row_tile(m, 512)
    return pl.pallas_call(
        _proj_res_kernel,
        out_shape=jax.ShapeDtypeStruct((m, d), F32),
        grid=(m // tm,),
        in_specs=[_row_spec(tm, a.shape[1]), _row_spec(tm, d), _full_spec(w.shape)],
        out_specs=_row_spec(tm, d),
        compiler_params=_params(("parallel",)),
        name="proj_res",
    )(a, x, w)


def _conv_in_kernel(x_ref, g_ref, wa_ref, wb_ref, o_ref):
    xn = _rms(x_ref[...], g_ref[...]).astype(BF16)
    o_ref[...] = _dot(xn, wa_ref[...]) * jax.nn.sigmoid(_dot(xn, wb_ref[...]))


def conv_in(x, g, w_a, w_b):
    m, d = x.shape
    tm = _row_tile(m, 512)
    return pl.pallas_call(
        _conv_in_kernel,
        out_shape=jax.ShapeDtypeStruct((m, d), F32),
        grid=(m // tm,),
        in_specs=[_row_spec(tm, d), _full_spec((1, d)), _full_spec(w_a.shape), _full_spec(w_b.shape)],
        out_specs=_row_spec(tm, d),
        compiler_params=_params(("parallel",)),
        name="conv_in",
    )(x, g.reshape(1, d), w_a, w_b)


def _ln_silu_proj(h, lng, lnb, w):
    mu = jnp.mean(h, axis=-1, keepdims=True)
    hc = h - mu
    var = jnp.mean(hc * hc, axis=-1, keepdims=True)
    hn = hc * lax.rsqrt(var + LN_EPS) * lng + lnb
    hn = hn * jax.nn.sigmoid(hn)
    return _dot(hn.astype(BF16), w)


CONV_HALO = 32
CONV_ROWS = 16


def _conv_seq_kernel(u_ref, halo_ref, x_ref, dw_ref, dwb_ref, lng_ref, lnb_ref, w_ref, o_ref,
                     up_ref, h_ref, *, tm):
    i = pl.program_id(1)
    up_ref[0, 0:CONV_HALO, :] = jnp.where(i > 0, halo_ref[0], 0.0)
    up_ref[0, CONV_HALO:, :] = u_ref[0]
    n_sh = tm + CONV_HALO - SUBLANES
    for s in range(1, SUBLANES):
        up_ref[s, 0:n_sh, :] = up_ref[0, s:s + n_sh, :]
    lead = CONV_HALO - (CONV_WIDTH - 1)
    for c in range(tm // CONV_ROWS):
        r0 = c * CONV_ROWS
        acc = jnp.zeros((CONV_ROWS, D_MODEL), F32)
        for j in range(CONV_WIDTH):
            s, a0 = (lead + j) % SUBLANES, r0 + (lead + j) // SUBLANES * SUBLANES
            acc = acc + up_ref[s, a0:a0 + CONV_ROWS, :] * dw_ref[j:j + 1, :]
        h_ref[r0:r0 + CONV_ROWS, :] = acc
    h = h_ref[...] + dwb_ref[...]
    o_ref[0] = x_ref[0] + _ln_silu_proj(h, lng_ref[...], lnb_ref[...], w_ref[...])


def conv_seq(u, x, dw, dwb, lng, lnb, w_out):
    b, s, d = u.shape
    tm = 128
    nt = s // tm
    hb = tm // CONV_HALO
    vec = lambda v: v.reshape(1, d)
    return pl.pallas_call(
        functools.partial(_conv_seq_kernel, tm=tm),
        out_shape=jax.ShapeDtypeStruct((b, s, d), F32),
        grid=(b, nt),
        in_specs=[
            pl.BlockSpec((1, tm, d), lambda bi, i: (bi, i, 0)),
            pl.BlockSpec((1, CONV_HALO, d), lambda bi, i: (bi, jnp.maximum(i * hb - 1, 0), 0)),
            pl.BlockSpec((1, tm, d), lambda bi, i: (bi, i, 0)),
            pl.BlockSpec((CONV_WIDTH, d), lambda bi, i: (0, 0)),
            pl.BlockSpec((1, d), lambda bi, i: (0, 0)),
            pl.BlockSpec((1, d), lambda bi, i: (0, 0)),
            pl.BlockSpec((1, d), lambda bi, i: (0, 0)),
            pl.BlockSpec((d, d), lambda bi, i: (0, 0)),
        ],
        out_specs=pl.BlockSpec((1, tm, d), lambda bi, i: (bi, i, 0)),
        scratch_shapes=[pltpu.VMEM((SUBLANES, tm + CONV_HALO, d), F32), pltpu.VMEM((tm, d), F32)],
        compiler_params=_params(("parallel", "arbitrary")),
        name="conv_seq",
    )(u, u, x, dw, vec(dwb), vec(lng), vec(lnb), w_out)


def _conv_step_kernel(u_ref, buf_ref, x_ref, dw_ref, dwb_ref, lng_ref, lnb_ref, w_ref, o_ref):
    nb = CONV_WIDTH - 1
    h = jnp.sum(buf_ref[...] * dw_ref[0:nb, :][None], axis=1)
    h = h + u_ref[...] * dw_ref[nb:nb + 1, :] + dwb_ref[...]
    o_ref[...] = x_ref[...] + _ln_silu_proj(h, lng_ref[...], lnb_ref[...], w_ref[...])


def conv_step(u, buf, x, dw, dwb, lng, lnb, w_out):
    b, d = u.shape
    tb = 32
    vec = lambda v: v.reshape(1, d)
    return pl.pallas_call(
        _conv_step_kernel,
        out_shape=jax.ShapeDtypeStruct((b, d), F32),
        grid=(b // tb,),
        in_specs=[
            _row_spec(tb, d),
            pl.BlockSpec((tb, CONV_WIDTH - 1, d), lambda i: (i, 0, 0)),
            _row_spec(tb, d),
            _full_spec((CONV_WIDTH, d)), _full_spec((1, d)), _full_spec((1, d)), _full_spec((1, d)),
            _full_spec((d, d)),
        ],
        out_specs=_row_spec(tb, d),
        compiler_params=_params(("parallel",)),
        name="conv_step",
    )(u, buf, x, dw, vec(dwb), vec(lng), vec(lnb), w_out)


POOL_HALO = 16


def _pool_project(d, w_ref, scale):
    ys = []
    for gi in range(len(POOL_WINDOWS)):
        dg = d[:, gi * POOL_GROUP:(gi + 1) * POOL_GROUP].astype(BF16)
        ys.append(_dot(dg, w_ref[gi]))
    return jnp.concatenate(ys, axis=-1) * scale


def _pool_seq_kernel(x_ref, halo_ref, g_ref, w_ref, sc_ref, o_ref, tail_ref, xc_ref, *, tm, nt):
    i = pl.program_id(1)
    g = g_ref[...]
    xn = _rms(x_ref[0], g)
    xc_ref[0:POOL_HALO, :] = jnp.where(i > 0, _rms(halo_ref[0], g), 0.0)
    xc_ref[POOL_HALO:, :] = xn
    pos = i * tm + lax.broadcasted_iota(jnp.int32, (tm, 1), 0)
    means = []
    for gi, w in enumerate(POOL_WINDOWS):
        c = slice(gi * POOL_GROUP, (gi + 1) * POOL_GROUP)
        win = xn[:, c]
        for k in range(1, w):
            win = win + xc_ref[POOL_HALO - k:POOL_HALO - k + tm, c]
        cnt = jnp.minimum(w, pos + 1).astype(F32)
        means.append(win / cnt)
    d = jnp.concatenate(means, axis=-1) - xn
    o_ref[0] = x_ref[0] + _pool_project(d, w_ref, sc_ref[...])

    @pl.when(i == nt - 1)
    def _():
        tail_ref[0] = xn[tm - POOL_HALO:, :]


def pool_seq(x, g, w_grp, scale):
    b, s, d = x.shape
    tm = 256
    nt = s // tm
    hb = tm // POOL_HALO
    return pl.pallas_call(
        functools.partial(_pool_seq_kernel, tm=tm, nt=nt),
        out_shape=(jax.ShapeDtypeStruct((b, s, d), F32), jax.ShapeDtypeStruct((b, POOL_HALO, d), F32)),
        grid=(b, nt),
        in_specs=[
            pl.BlockSpec((1, tm, d), lambda bi, i: (bi, i, 0)),
            pl.BlockSpec((1, POOL_HALO, d), lambda bi, i: (bi, jnp.maximum(i * hb - 1, 0), 0)),
            pl.BlockSpec((1, d), lambda bi, i: (0, 0)),
            pl.BlockSpec(w_grp.shape, lambda bi, i: (0, 0, 0)),
            pl.BlockSpec((1, d), lambda bi, i: (0, 0)),
        ],
        out_specs=(pl.BlockSpec((1, tm, d), lambda bi, i: (bi, i, 0)),
                   pl.BlockSpec((1, POOL_HALO, d), lambda bi, i: (bi, 0, 0))),
        scratch_shapes=[pltpu.VMEM((tm + POOL_HALO, d), F32)],
        compiler_params=_params(("parallel", "arbitrary")),
        name="pool_seq",
    )(x, x, g.reshape(1, d), w_grp, scale.reshape(1, d))


def _pool_step_kernel(x_ref, buf_ref, g_ref, msk_ref, icnt_ref, w_ref, sc_ref, o_ref, xn_ref):
    xn = _rms(x_ref[...], g_ref[...])
    win = xn + jnp.sum(buf_ref[...] * msk_ref[...][None], axis=1)
    d = win * icnt_ref[...] - xn
    o_ref[...] = x_ref[...] + _pool_project(d, w_ref, sc_ref[...])
    xn_ref[...] = xn


def pool_step(x, buf, start_pos, g, w_grp, scale):
    b, d = x.shape
    tb = 32
    rows = jnp.arange(POOL_BUF)[:, None]
    win_of_lane = jnp.repeat(jnp.asarray(POOL_WINDOWS), POOL_GROUP)[None, :]
    msk = (rows >= POOL_BUF - (win_of_lane - 1)).astype(F32)
    icnt = 1.0 / jnp.minimum(win_of_lane, start_pos + 1).astype(F32)
    return pl.pallas_call(
        _pool_step_kernel,
        out_shape=(jax.ShapeDtypeStruct((b, d), F32), jax.ShapeDtypeStruct((b, d), F32)),
        grid=(b // tb,),
        in_specs=[
            _row_spec(tb, d),
            pl.BlockSpec((tb, POOL_BUF, d), lambda i: (i, 0, 0)),
            _full_spec((1, d)), _full_spec((POOL_BUF, d)), _full_spec((1, d)),
            _full_spec(w_grp.shape), _full_spec((1, d)),
        ],
        out_specs=(_row_spec(tb, d), _row_spec(tb, d)),
        compiler_params=_params(("parallel",)),
        name="pool_step",
    )(x, buf, g.reshape(1, d), msk, icnt, w_grp, scale.reshape(1, d))


def _qkv_kernel(x_ref, g_ref, wq_ref, wk_ref, wv_ref, q_ref, k_ref, v_ref, kb_ref, vb_ref):
    xn = _rms(x_ref[...], g_ref[...]).astype(BF16)
    q_ref[...] = (_dot(xn, wq_ref[...]) * (SB_SCALE * LOG2E)).astype(BF16)
    k = _dot(xn, wk_ref[...])
    v = _dot(xn, wv_ref[...])
    k_ref[...] = k
    v_ref[...] = v
    kb_ref[...] = k.astype(BF16)
    vb_ref[...] = v.astype(BF16)


def qkv_proj(x, g, wq, wk, wv):
    m, d = x.shape
    tm = _row_tile(m, 512)
    sd = lambda dt: jax.ShapeDtypeStruct((m, d), dt)
    return pl.pallas_call(
        _qkv_kernel,
        out_shape=(sd(BF16), sd(F32), sd(F32), sd(BF16), sd(BF16)),
        grid=(m // tm,),
        in_specs=[_row_spec(tm, d), _full_spec((1, d)), _full_spec((d, d)), _full_spec((d, d)),
                  _full_spec((d, d))],
        out_specs=tuple(_row_spec(tm, d) for _ in range(5)),
        compiler_params=_params(("parallel",)),
        name="qkv_proj",
    )(x, g.reshape(1, d), wq, wk, wv)


def _softplus(z):
    return jnp.maximum(z, 0.0) + jnp.log(1.0 + jnp.exp(-jnp.abs(z)))


ATT_BLK = 256
ATT_Q = 512
ATT_ROWS = 128
ATT_GROUP = 4
LOG2E = 1.4426950408889634


def _softplus2(z):
    neg_abs = lax.bitcast_convert_type(
        lax.bitcast_convert_type(z, jnp.int32) | jnp.int32(-2 ** 31), F32)
    return jnp.maximum(z, 0.0) + jnp.log2(1.0 + jnp.exp2(neg_abs))


def _sb_prompt_kernel(bias_ref, q_ref, k_ref, v_ref, o_ref, z_scr, p_scr):
    hp = pl.program_id(1)
    i = pl.program_id(2)
    t = ATT_BLK
    nd = ATT_Q // t
    nr = ATT_Q // ATT_ROWS
    row = lax.broadcasted_iota(jnp.int32, (t, t), 0)
    col = lax.broadcasted_iota(jnp.int32, (t, t), 1)
    neg_upper = jnp.where(row > col, -1.0, 0.0).astype(BF16)
    q_off = lax.broadcasted_iota(jnp.int32, (ATT_ROWS, t), 0)
    k_off = lax.broadcasted_iota(jnp.int32, (ATT_ROWS, t), 1)
    heads = [slice(hh * HEAD_DIM, (hh + 1) * HEAD_DIM) for hh in range(2)]
    rows = [slice(r * ATT_ROWS, (r + 1) * ATT_ROWS) for r in range(nr)]
    chains = [(hh, r) for hh in range(2) for r in range(nr)]
    nc = len(chains)
    bias = [bias_ref[2 * hp + hh] * LOG2E for hh in range(2)]
    q = [q_ref[0, rows[r], heads[hh]] for hh, r in chains]

    def logits(blk):
        kb = k_ref[0, pl.ds(pl.multiple_of(blk * t, t), t), :]
        return [_dot_nt(q[ci], kb[:, heads[hh]]) + bias[hh] for ci, (hh, r) in enumerate(chains)]

    def stash_logits(zs):
        for ci in range(nc):
            z_scr[ci] = zs[ci]

    def stash_weights(zs, cs, masks):
        cs_new = []
        for g0 in range(0, nc, ATT_GROUP):
            group = range(g0, g0 + ATT_GROUP)
            sps, zsp = [], []
            for ci in group:
                r = chains[ci][1]
                sp = _softplus2(zs[ci])
                sps.append(sp if masks[r] is None else jnp.where(masks[r], sp, 0.0))
                zsp.append(zs[ci] - sp)
            betweens = [_dot(sp.astype(BF16), neg_upper) for sp in sps]
            for ci, x, bt, sp in zip(group, zsp, betweens, sps):
                r = chains[ci][1]
                p = jnp.exp2(x + bt + cs[ci])
                p_scr[ci] = (p if masks[r] is None else jnp.where(masks[r], p, 0.0)).astype(BF16)
                cs_new.append(cs[ci] + bt[:, 0:1] - sp[:, 0:1])
        return cs_new

    def weighted_values(blk, accs):
        vb = v_ref[0, pl.ds(pl.multiple_of(blk * t, t), t), :]
        return [acc + _dot(p_scr[ci], vb[:, heads[hh]]) for ci, (acc, (hh, r)) in enumerate(zip(accs, chains))]

    cs = [jnp.zeros((ATT_ROWS, 1), F32)] * nc
    accs = [jnp.zeros((ATT_ROWS, HEAD_DIM), F32)] * nc
    zs = logits(nd * i + nd - 1)
    for dblk in reversed(range(nd)):
        blk = nd * i + dblk
        z_next = logits(jnp.maximum(blk - 1, 0))
        if dblk < nd - 1:
            accs = weighted_values(blk + 1, accs)
        masks = [(dblk * t + k_off) < (r * ATT_ROWS + q_off) for r in range(nr)]
        cs = stash_weights(zs, cs, masks)
        zs = z_next
    stash_logits(zs)

    def body(n, carry):
        cs, accs = carry
        blk = nd * i - 1 - n
        zs = [z_scr[ci] for ci in range(nc)]
        stash_logits(logits(jnp.maximum(blk - 1, 0)))
        accs = weighted_values(blk + 1, accs)
        cs = stash_weights(zs, cs, [None] * nr)
        return cs, accs

    cs, accs = lax.fori_loop(0, nd * i, body, (cs, accs))
    accs = weighted_values(0, accs)
    o_ref[0] = jnp.concatenate(
        [jnp.concatenate([accs[hh * nr + r] for r in range(nr)], axis=0) for hh in range(2)], axis=-1)


def sb_attn_prompt(q, k, v, bias):
    b, s, d = q.shape
    nc = 2 * ATT_Q // ATT_ROWS
    return pl.pallas_call(
        _sb_prompt_kernel,
        out_shape=jax.ShapeDtypeStruct((b, s, d), F32),
        grid=(b, N_HEADS // 2, s // ATT_Q),
        in_specs=[
            pl.BlockSpec(memory_space=pltpu.SMEM),
            pl.BlockSpec((1, ATT_Q, LANES), lambda bi, hp, i: (bi, i, hp)),
            pl.BlockSpec((1, s, LANES), lambda bi, hp, i: (bi, 0, hp)),
            pl.BlockSpec((1, s, LANES), lambda bi, hp, i: (bi, 0, hp)),
        ],
        out_specs=pl.BlockSpec((1, ATT_Q, LANES), lambda bi, hp, i: (bi, i, hp)),
        scratch_shapes=[pltpu.VMEM((nc, ATT_ROWS, ATT_BLK), F32), pltpu.VMEM((nc, ATT_ROWS, ATT_BLK), BF16)],
        compiler_params=_params(("parallel", "parallel", "arbitrary")),
        name="sb_attn_prompt",
    )(bias, q, k, v)


DEC_PAGES = 8


def _sb_decode_kernel(pt_ref, q_ref, kn_ref, vn_ref, bias_ref, *refs, n_steps):
    del pt_ref
    kp_refs, vp_refs = refs[:DEC_PAGES], refs[DEC_PAGES:2 * DEC_PAGES]
    o_ref, c_ref, qcol_ref, acc_ref, onew_ref = refs[2 * DEC_PAGES:]
    p = pl.program_id(1)
    t = PAGE_SIZE
    dh = HEAD_DIM
    bias = bias_ref[...] * LOG2E
    eye = (lax.broadcasted_iota(jnp.int32, (1, dh, dh), 1) == lax.broadcasted_iota(jnp.int32, (1, dh, dh), 2))

    @pl.when(p == 0)
    def _():
        q = q_ref[0]
        zn = jnp.sum(kn_ref[0] * q, axis=-1, keepdims=True) + bias
        q_pos = n_steps * DEC_PAGES * PAGE_SIZE
        visible = (q_pos + lax.broadcasted_iota(jnp.int32, zn.shape, 1)) < q_pos
        spn = _softplus2(zn)
        c_ref[...] = jnp.where(visible, -spn, 0.0)
        onew_ref[...] = jnp.where(visible, jnp.exp2(zn - spn), 0.0) * vn_ref[0]
        acc_ref[...] = jnp.zeros_like(acc_ref)
        q_col = jnp.sum(jnp.where(eye, q[:, None, :], 0.0), axis=-1, keepdims=True)
        qcol_ref[...] = jnp.broadcast_to(q_col, qcol_ref.shape)

    row = lax.broadcasted_iota(jnp.int32, (t, t), 0)
    col = lax.broadcasted_iota(jnp.int32, (t, t), 1)
    neg_upper = jnp.where(row > col, -1.0, 0.0).astype(BF16)
    c = c_ref[...]
    q_col = qcol_ref[...]
    for g in range(DEC_PAGES):
        z = jnp.sum(kp_refs[g][0] * q_col, axis=1) + bias
        sp = _softplus2(z)
        between = _dot(sp.astype(BF16), neg_upper)
        a = jnp.exp2(z - sp + between + c)
        acc_ref[...] += vp_refs[g][0] * a[:, None, :]
        c = c + between[:, 0:1] - sp[:, 0:1]
    c_ref[...] = c

    @pl.when(p == n_steps - 1)
    def _():
        o_col = jnp.sum(acc_ref[...], axis=-1, keepdims=True)
        o_ref[0] = jnp.sum(jnp.where(eye, o_col, 0.0), axis=1) + onew_ref[...]


def sb_attn_decode(q, k_new, v_new, bias, cache_k, cache_v, page_table):
    b = q.shape[0]
    n_pages = page_table.shape[1]
    n_steps = n_pages // DEC_PAGES
    hd = (N_HEADS, HEAD_DIM)

    def page(g):
        return lambda bi, p, pt: (pt[bi * n_pages + (n_pages - 1 - (p * DEC_PAGES + g))], 0, 0, 0)

    row_spec = pl.BlockSpec((1,) + hd, lambda bi, p, pt: (bi, 0, 0))
    page_specs = [pl.BlockSpec((1,) + hd + (PAGE_SIZE,), page(g)) for g in range(DEC_PAGES)]
    return pl.pallas_call(
        functools.partial(_sb_decode_kernel, n_steps=n_steps),
        out_shape=jax.ShapeDtypeStruct((b,) + hd, F32),
        grid_spec=pltpu.PrefetchScalarGridSpec(
            num_scalar_prefetch=1,
            grid=(b, n_steps),
            in_specs=[row_spec, row_spec, row_spec, pl.BlockSpec((N_HEADS, 1), lambda bi, p, pt: (0, 0))]
            + page_specs + page_specs,
            out_specs=row_spec,
            scratch_shapes=[pltpu.VMEM((N_HEADS, 1), F32), pltpu.VMEM(hd + (PAGE_SIZE,), F32),
                            pltpu.VMEM(hd + (PAGE_SIZE,), F32), pltpu.VMEM(hd, F32)],
        ),
        compiler_params=_params(("parallel", "arbitrary")),
        name="sb_attn_decode",
    )(page_table.reshape(-1), q, k_new, v_new, bias.reshape(N_HEADS, 1),
      *([cache_k] * DEC_PAGES), *([cache_v] * DEC_PAGES))


def _head_sum(x, sel, expand):
    return _dot_x2(_dot_x2(x, sel), expand)


def _rwkv_in_kernel(x_ref, prev_ref, g_ref, mix_ref, wr_ref, wk_ref, wv_ref, w0_ref, w1_ref, w2_ref,
                    a0_ref, a1_ref, a2_ref, g1_ref, g2_ref, kk_ref, ka_ref, sel_ref, exp_ref,
                    r_o, lw_o, k_o, v_o, kk_o, a_o, g_o, xn_o):
    xn = _rms(x_ref[...], g_ref[...])
    xx = prev_ref[...] - xn
    mixed = lambda c: (xn + xx * mix_ref[c:c + 1, :]).astype(BF16)
    xr, xw, xk, xv, xa, xg = (mixed(c) for c in range(6))
    r = _dot(xr, wr_ref[...])
    w = w0_ref[...] + _dot(jnp.tanh(_dot(xw, w1_ref[...])).astype(BF16), w2_ref[...])
    w = -_softplus(-w) - 0.5
    k = _dot(xk, wk_ref[...])
    v = _dot(xv, wv_ref[...])
    a = jax.nn.sigmoid(a0_ref[...] + _dot(_dot(xa, a1_ref[...]).astype(BF16), a2_ref[...]))
    g = _dot(jax.nn.sigmoid(_dot(xg, g1_ref[...])).astype(BF16), g2_ref[...])
    kk = k * kk_ref[...]
    nrm = jnp.sqrt(_head_sum(kk * kk, sel_ref[...], exp_ref[...]))
    kk = kk / jnp.maximum(nrm, 1e-12)
    r_o[...] = r
    lw_o[...] = -jnp.exp(w)
    k_o[...] = k * (1.0 + (a - 1.0) * ka_ref[...])
    v_o[...] = v
    kk_o[...] = kk
    a_o[...] = a
    g_o[...] = g
    xn_o[...] = xn


def _head_selectors():
    lane = jnp.arange(D_MODEL)[:, None] // HEAD_DIM
    sel = (lane == jnp.arange(LANES)[None, :]).astype(BF16)
    return sel, sel.T


def rwkv_in(x, x_prev_rows, g, p):
    m, d = x.shape
    tm = _row_tile(m, 256)
    sel, expand = _head_selectors()
    vec = lambda v: v.reshape(1, d)
    ws = [p["mix"], p["w_r"], p["w_k"], p["w_v"], vec(p["w0"]), p["w1"], p["w2"], vec(p["a0"]), p["a1"],
          p["a2"], p["g1"], p["g2"], vec(p["k_k"]), vec(p["k_a"]), sel, expand]
    sd = jax.ShapeDtypeStruct((m, d), F32)
    return pl.pallas_call(
        _rwkv_in_kernel,
        out_shape=(sd,) * 8,
        grid=(m // tm,),
        in_specs=[_row_spec(tm, d), _row_spec(tm, d), _full_spec((1, d))] + [_full_spec(w.shape) for w in ws],
        out_specs=tuple(_row_spec(tm, d) for _ in range(8)),
        compiler_params=_params(("parallel",)),
        name="rwkv_in",
    )(x, x_prev_rows, vec(g), *ws)


def _norm_shift_kernel(x_ref, halo_ref, g_ref, o_ref, *, tm):
    i = pl.program_id(1)
    g = g_ref[...]
    xn = _rms(x_ref[0], g)
    last = _rms(halo_ref[0], g)[7:8, :]
    first = jnp.where(i > 0, last, 0.0)
    rolled = pltpu.roll(xn, 1, axis=0)
    row = lax.broadcasted_iota(jnp.int32, (tm, 1), 0)
    o_ref[0] = jnp.where(row == 0, first, rolled)


def norm_shift(x, g):
    b, s, d = x.shape
    tm = 512
    hb = tm // 8
    return pl.pallas_call(
        functools.partial(_norm_shift_kernel, tm=tm),
        out_shape=jax.ShapeDtypeStruct((b, s, d), F32),
        grid=(b, s // tm),
        in_specs=[
            pl.BlockSpec((1, tm, d), lambda bi, i: (bi, i, 0)),
            pl.BlockSpec((1, 8, d), lambda bi, i: (bi, jnp.maximum(i * hb - 1, 0), 0)),
            pl.BlockSpec((1, d), lambda bi, i: (0, 0)),
        ],
        out_specs=pl.BlockSpec((1, tm, d), lambda bi, i: (bi, i, 0)),
        compiler_params=_params(("parallel", "arbitrary")),
        name="norm_shift",
    )(x, x, g.reshape(1, d))


WKV_CHUNK = 64


def _wkv_seq_kernel(r_ref, lw_ref, k_ref, v_ref, kk_ref, a_ref, o_ref, s_ref, h_ref, *, tb, nt):
    i = pl.program_id(1)
    c = WKV_CHUNK
    dh = HEAD_DIM

    @pl.when(i == 0)
    def _():
        h_ref[...] = jnp.zeros_like(h_ref)

    row = lax.broadcasted_iota(jnp.int32, (c, c), 0)
    col = lax.broadcasted_iota(jnp.int32, (c, c), 1)
    tri_incl = _ones_where(col <= row)
    eye = row == col
    row2 = lax.broadcasted_iota(jnp.int32, (2 * c, 2 * c), 0)
    col2 = lax.broadcasted_iota(jnp.int32, (2 * c, 2 * c), 1)
    colm = jnp.where(col2 < c, col2, col2 - c)
    gmask = colm < jnp.where(row2 < c, row2, row2 - c + 1)

    def chunk(n, carry):
        t0 = pl.multiple_of(n * c, c)
        sl = pl.ds(t0, c)
        lw = lw_ref[0, sl, :]
        cl = _dot_x3(tri_incl, lw)
        kk = kk_ref[0, sl, :]
        cl_end = cl[c - 1:c, :]
        g_end = jnp.exp(cl_end)
        inv = jnp.exp(-cl)
        at = (-kk * jnp.exp(cl - lw)).astype(BF16)
        rt_f = r_ref[0, sl, :] * jnp.exp(cl)
        rt = rt_f.astype(BF16)
        b = kk * a_ref[0, sl, :]
        k = k_ref[0, sl, :]
        bt = (b * inv).astype(BF16)
        kt = (k * inv).astype(BF16)
        to_end = jnp.exp(cl_end - cl)
        be = (b * to_end).astype(BF16)
        ke = (k * to_end).astype(BF16)
        vb = v_ref[0, sl, :].astype(BF16)
        heads = [slice(h * dh, (h + 1) * dh) for h in range(N_HEADS)]
        gms = [jnp.where(gmask, _dot_nt(jnp.concatenate([at[:, ls], rt[:, ls]], axis=0),
                                        jnp.concatenate([bt[:, ls], kt[:, ls]], axis=0)), 0.0)
               for ls in heads]
        mabs = [gm[0:c, 0:c] for gm in gms]
        mvs = [_dot(gm[0:c, c:].astype(BF16), vb[:, ls]) for gm, ls in zip(gms, heads)]
        pws = [m.astype(BF16) for m in mabs]
        tms = [jnp.where(eye, 1.0, m) for m in mabs]
        for _ in range(5):
            pws = [_dot(pw, pw).astype(BF16) for pw in pws]
            tms = [tm_ + _dot(tm_.astype(BF16), pw) for tm_, pw in zip(tms, pws)]
        aws = [_dot(tm_.astype(BF16), jnp.concatenate([at[:, ls], mv.astype(BF16)], axis=1))
               for tm_, mv, ls in zip(tms, mvs, heads)]
        zmats = [jnp.concatenate([aw.astype(BF16),
                                  jnp.concatenate([jnp.zeros((c, dh), BF16), vb[:, ls]], axis=1)], axis=0)
                 for aw, ls in zip(aws, heads)]
        x1s = [_dot(gm[c:, :].astype(BF16), zm) for gm, zm in zip(gms, zmats)]
        x2s = [_dot_tn(jnp.concatenate([be[:, ls], ke[:, ls]], axis=0), zm)
               for zm, ls in zip(zmats, heads)]
        ress = []
        for h, (x1, x2, ls) in enumerate(zip(x1s, x2s, heads)):
            dmat = jnp.where(eye, jnp.broadcast_to(g_end[:, ls], (c, dh)), 0.0)
            lb = jnp.concatenate([rt_f[:, ls] + x1[:, 0:dh], dmat + x2[:, 0:dh]], axis=0).astype(BF16)
            hi, lo = _split2(h_ref[h])
            ress.append(_dot(lb, hi) + _dot(lb, lo))
        for h, (res, x2) in enumerate(zip(ress, x2s)):
            h_ref[h] = res[c:, :] + x2[:, dh:]
        o_ref[0, sl, :] = jnp.concatenate([res[0:c, :] + x1[:, dh:] for res, x1 in zip(ress, x1s)], axis=-1)
        return carry

    lax.fori_loop(0, tb // c, chunk, 0)

    @pl.when(i == nt - 1)
    def _():
        s_ref[0] = h_ref[...]


def wkv_seq(r, lw, k, v, kk, a):
    b, s, d = r.shape
    tb = 256
    nt = s // tb
    blk = pl.BlockSpec((1, tb, d), lambda bi, i: (bi, i, 0))
    return pl.pallas_call(
        functools.partial(_wkv_seq_kernel, tb=tb, nt=nt),
        out_shape=(jax.ShapeDtypeStruct((b, s, d), F32),
                   jax.ShapeDtypeStruct((b, N_HEADS, HEAD_DIM, HEAD_DIM), F32)),
        grid=(b, nt),
        in_specs=[blk] * 6,
        out_specs=(blk, pl.BlockSpec((1, N_HEADS, HEAD_DIM, HEAD_DIM), lambda bi, i: (bi, 0, 0, 0))),
        scratch_shapes=[pltpu.VMEM((N_HEADS, HEAD_DIM, HEAD_DIM), F32)],
        compiler_params=_params(("parallel", "arbitrary")),
        name="wkv_seq",
    )(r, lw, k, v, kk, a)


def _wkv_step_kernel(r_ref, lw_ref, k_ref, v_ref, kk_ref, a_ref, s0_ref, o_ref, s_ref, *, tb):
    dh = HEAD_DIM
    eye = (lax.broadcasted_iota(jnp.int32, (1, dh, dh), 1) == lax.broadcasted_iota(jnp.int32, (1, dh, dh), 2))
    bs = range(tb)
    along_keys = lambda ref, bi: ref[bi][:, None, :]
    s0 = [s0_ref[bi] for bi in bs]
    kk = [along_keys(kk_ref, bi) for bi in bs]
    v_col = [jnp.sum(jnp.where(eye, along_keys(v_ref, bi), 0.0), axis=-1, keepdims=True) for bi in bs]
    sa = [jnp.sum(s0[bi] * -kk[bi], axis=-1, keepdims=True) for bi in bs]
    s1 = [s0[bi] * jnp.exp(along_keys(lw_ref, bi)) + sa[bi] * (kk[bi] * along_keys(a_ref, bi))
          + v_col[bi] * along_keys(k_ref, bi) for bi in bs]
    for bi in bs:
        s_ref[bi] = s1[bi]
    o_col = [jnp.sum(s1[bi] * along_keys(r_ref, bi), axis=-1, keepdims=True) for bi in bs]
    for bi in bs:
        o_ref[bi] = jnp.sum(jnp.where(eye, o_col[bi], 0.0), axis=1)


def wkv_step(r, lw, k, v, kk, a, s0):
    b = r.shape[0]
    tb = 8
    hd = (N_HEADS, HEAD_DIM)
    rows = pl.BlockSpec((tb,) + hd, lambda i: (i, 0, 0))
    st = pl.BlockSpec((tb,) + hd + (HEAD_DIM,), lambda i: (i, 0, 0, 0))
    return pl.pallas_call(
        functools.partial(_wkv_step_kernel, tb=tb),
        out_shape=(jax.ShapeDtypeStruct((b,) + hd, F32), jax.ShapeDtypeStruct(s0.shape, F32)),
        grid=(b // tb,),
        in_specs=[rows] * 6 + [st],
        out_specs=(rows, st),
        compiler_params=_params(("parallel",)),
        name="wkv_step",
    )(r, lw, k, v, kk, a, s0)


def _rwkv_out_kernel(o_ref, r_ref, k_ref, v_ref, g_ref, x_ref, rk_ref, lng_ref, lnb_ref, sel_ref, exp_ref,
                     w_ref, y_ref):
    sel, expand = sel_ref[...], exp_ref[...]
    o = o_ref[...]
    mu = _head_sum(o, sel, expand) * (1.0 / HEAD_DIM)
    oc = o - mu
    var = _head_sum(oc * oc, sel, expand) * (1.0 / HEAD_DIM)
    on = oc * lax.rsqrt(var + GN_EPS) * lng_ref[...] + lnb_ref[...]
    bonus = _head_sum(r_ref[...] * k_ref[...] * rk_ref[...], sel, expand)
    on = on + bonus * v_ref[...]
    y_ref[...] = x_ref[...] + _dot((on * g_ref[...]).astype(BF16), w_ref[...])


def rwkv_out(o, r, k, v, g, x, r_k, ln_g, ln_b, w_o):
    m, d = x.shape
    tm = _row_tile(m, 256)
    sel, expand = _head_selectors()
    vec = lambda t: t.reshape(1, d)
    return pl.pallas_call(
        _rwkv_out_kernel,
        out_shape=jax.ShapeDtypeStruct((m, d), F32),
        grid=(m // tm,),
        in_specs=[_row_spec(tm, d)] * 6 + [_full_spec((1, d))] * 3
        + [_full_spec(sel.shape), _full_spec(expand.shape), _full_spec((d, d))],
        out_specs=_row_spec(tm, d),
        compiler_params=_params(("parallel",)),
        name="rwkv_out",
    )(o, r, k, v, g, x, vec(r_k), vec(ln_g), vec(ln_b), sel, expand, w_o)


def _trunk(x, seq, start_pos, conv_buf, pool_buf, kv_past, wkv0, shift0, w):
    b, t, d = x.shape
    m = b * t
    flat = lambda z: z.reshape(m, d)
    x2 = flat(x)

    u = conv_in(x2, w["norm_mix"][0], w["conv_wa"], w["conv_wb"])
    cw = (w["conv_dw"], w["conv_dw_b"], w["conv_ln_g"], w["conv_ln_b"], w["conv_w_out"])
    if seq:
        x2 = flat(conv_seq(u.reshape(b, t, d), x, *cw))
        new_conv = u.reshape(b, t, d)[:, t - (CONV_WIDTH - 1):]
    else:
        x2 = conv_step(u, conv_buf, x2, *cw)
        new_conv = jnp.concatenate([conv_buf[:, 1:], u[:, None]], axis=1)
    x2 = mlp(x2, w["norm_mlp"][0], w["mlp_up"][0], w["mlp_down"][0])

    if seq:
        y, tail = pool_seq(x2.reshape(b, t, d), w["norm_mix"][1], w["pool_w"], w["pool_scale"])
        x2 = flat(y)
        new_pool = tail[:, POOL_HALO - POOL_BUF:]
    else:
        x2, xn = pool_step(x2, pool_buf, start_pos, w["norm_mix"][1], w["pool_w"], w["pool_scale"])
        new_pool = jnp.concatenate([pool_buf[:, 1:], xn[:, None]], axis=1)
    x2 = mlp(x2, w["norm_mlp"][1], w["mlp_up"][1], w["mlp_down"][1])

    q, k, v, kb, vb = qkv_proj(x2, w["norm_mix"][2], w["attn_wq"], w["attn_wk"], w["attn_wv"])
    if seq:
        att = sb_attn_prompt(q.reshape(b, t, d), kb.reshape(b, t, d), vb.reshape(b, t, d), w["attn_sb_bias"])
        att = flat(att)
    else:
        cache_k, cache_v, page_table = kv_past
        hd = lambda z: z.astype(F32).reshape(b, N_HEADS, HEAD_DIM)
        att = sb_attn_decode(hd(q), hd(k), hd(v), w["attn_sb_bias"], cache_k, cache_v, page_table)
        att = att.reshape(b, d)
    x2 = proj_res(att, x2, w["attn_w_o"])
    new_k = k.reshape(b, t, N_HEADS, HEAD_DIM)
    new_v = v.reshape(b, t, N_HEADS, HEAD_DIM)
    x2 = mlp(x2, w["norm_mlp"][2], w["mlp_up"][2], w["mlp_down"][2])

    if seq:
        prev = flat(norm_shift(x2.reshape(b, t, d), w["norm_mix"][3]))
    else:
        prev = shift0
    r, lw, k2, v2, kk, a, g, xn = rwkv_in(x2, prev, w["norm_mix"][3], w["rw"])
    if seq:
        sh = lambda z: z.reshape(b, t, d)
        o, hstate = wkv_seq(sh(r), sh(lw), sh(k2), sh(v2), sh(kk), sh(a))
        o = flat(o)
        new_wkv = jnp.swapaxes(hstate, -1, -2)
        new_shift = xn.reshape(b, t, d)[:, t - 1]
    else:
        hd = lambda z: z.reshape(b, N_HEADS, HEAD_DIM)
        o, new_wkv = wkv_step(hd(r), hd(lw), hd(k2), hd(v2), hd(kk), hd(a), wkv0)
        o = o.reshape(b, d)
        new_shift = xn
    x2 = rwkv_out(o, r, k2, v2, g, x2, w["rw"]["r_k"], w["rw"]["ln_g"], w["rw"]["ln_b"], w["rw"]["w_o"])
    x2 = mlp(x2, w["norm_mlp"][3], w["mlp_up"][3], w["mlp_down"][3], g_final=w["norm_final"])

    return x2.reshape(b, t, d), new_conv, new_pool, new_k, new_v, new_wkv, new_shift


def kernel(x_prompt, x_sample, cache_conv, cache_pool, cache_k, cache_v, page_table, state_wkv, state_shift,
           norm_mix, norm_mlp, norm_final, mlp_w_up, mlp_w_down,
           conv_w_in, conv_dw, conv_dw_b, conv_ln_g, conv_ln_b, conv_w_out,
           pool_w, pool_scale, attn_w_qkv, attn_w_o, attn_sb_bias,
           rw_mix, rw_w_r, rw_w_k, rw_w_v, rw_w_o, rw_w0, rw_w1, rw_w2, rw_a0, rw_a1, rw_a2,
           rw_g1, rw_g2, rw_k_k, rw_k_a, rw_r_k, rw_ln_g, rw_ln_b):
    d = D_MODEL
    bf = lambda z: z.astype(BF16)
    gl = rw_g1.shape[-1]
    glp = -(-gl // LANES) * LANES
    g1 = jnp.pad(rw_g1[0], ((0, 0), (0, glp - gl)))
    g2 = jnp.pad(rw_g2[0], ((0, glp - gl), (0, 0)))
    w = dict(
        norm_mix=norm_mix, norm_mlp=norm_mlp, norm_final=norm_final,
        mlp_up=bf(mlp_w_up), mlp_down=bf(mlp_w_down),
        conv_wa=bf(conv_w_in[0, :, :d]), conv_wb=bf(conv_w_in[0, :, d:]),
        conv_dw=conv_dw[0], conv_dw_b=conv_dw_b[0], conv_ln_g=conv_ln_g[0], conv_ln_b=conv_ln_b[0],
        conv_w_out=bf(conv_w_out[0]),
        pool_w=bf(pool_w[0]), pool_scale=pool_scale[0],
        attn_wq=bf(attn_w_qkv[0, :, :d]), attn_wk=bf(attn_w_qkv[0, :, d:2 * d]),
        attn_wv=bf(attn_w_qkv[0, :, 2 * d:]), attn_w_o=bf(attn_w_o[0]), attn_sb_bias=attn_sb_bias[0],
        rw=dict(mix=rw_mix[0], w_r=bf(rw_w_r[0]), w_k=bf(rw_w_k[0]), w_v=bf(rw_w_v[0]), w_o=bf(rw_w_o[0]),
                w0=rw_w0[0], w1=bf(rw_w1[0]), w2=bf(rw_w2[0]), a0=rw_a0[0], a1=bf(rw_a1[0]), a2=bf(rw_a2[0]),
                g1=bf(g1), g2=bf(g2), k_k=rw_k_k[0], k_a=rw_k_a[0], r_k=rw_r_k[0].reshape(-1),
                ln_g=rw_ln_g[0], ln_b=rw_ln_b[0]),
    )
    yp, conv_p, pool_p, k_p, v_p, wkv_p, shift_p = _trunk(
        x_prompt, True, 0, None, None, None, None, None, w)
    past_len = page_table.shape[1] * PAGE_SIZE
    ys, conv_s, pool_s, k_s, v_s, wkv_s, shift_s = _trunk(
        x_sample, False, past_len, cache_conv[0], cache_pool[0],
        (jnp.transpose(cache_k[0], (0, 2, 3, 1)), jnp.transpose(cache_v[0], (0, 2, 3, 1)), page_table),
        state_wkv[0], state_shift[0], w)
    st = lambda z: z[None]
    return (yp, ys, st(conv_p), st(conv_s), st(pool_p), st(pool_s), st(k_p), st(v_p), st(k_s), st(v_s),
            st(wkv_p), st(wkv_s), st(shift_p), st(shift_s))
```

```python
import functools

import jax
import jax.numpy as jnp
from jax import lax
from jax.experimental import pallas as pl
from jax.experimental.pallas import tpu as pltpu

F32 = jnp.float32
BF16 = jnp.bfloat16

D_MODEL = 1024
D_FF = 4 * D_MODEL
HEAD_DIM = 64
N_HEADS = D_MODEL // HEAD_DIM
CONV_WIDTH = 31
POOL_WINDOWS = (2, 4, 8, 16)
POOL_GROUP = D_MODEL // len(POOL_WINDOWS)
POOL_BUF = max(POOL_WINDOWS) - 1
PAGE_SIZE = 128
RMS_EPS = 1e-6
LN_EPS = 1e-5
GN_EPS = 1e-5 * HEAD_DIM
SB_SCALE = HEAD_DIM ** -0.5

LANES = 128
SUBLANES = 8
MXU_DIM = 256
VMEM_LIMIT = 48 << 20


def _params(sem, vmem=VMEM_LIMIT):
    return pltpu.CompilerParams(dimension_semantics=sem, vmem_limit_bytes=vmem)


def _rms(x, g):
    return x * lax.rsqrt(jnp.mean(x * x, axis=-1, keepdims=True) + RMS_EPS) * g


def _dot(a, b):
    return jnp.dot(a, b, preferred_element_type=F32)


def _dot_nt(a, b):
    return lax.dot_general(a, b, (((1,), (1,)), ((), ())), preferred_element_type=F32)


def _dot_tn(a, b):
    return lax.dot_general(a, b, (((0,), (0,)), ((), ())), preferred_element_type=F32)


def _ones_where(cond):
    return jnp.where(cond, 1.0, 0.0).astype(BF16)


def _split2(x):
    hi = x.astype(BF16)
    lo = (x - hi.astype(F32)).astype(BF16)
    return hi, lo


def _dot_x2(x, w):
    hi, lo = _split2(x)
    return _dot(hi, w) + _dot(lo, w)


def _dot_x3(w, x):
    hi = x.astype(BF16)
    r1 = x - hi.astype(F32)
    mid = r1.astype(BF16)
    lo = (r1 - mid.astype(F32)).astype(BF16)
    return _dot(w, hi) + _dot(w, mid) + _dot(w, lo)


def _row_spec(tm, n):
    return pl.BlockSpec((tm, n), lambda i: (i, 0))


def _full_spec(shape):
    return pl.BlockSpec(shape, lambda *_: (0,) * len(shape))


def _row_tile(m, pref):
    return pref if m % pref == 0 else m


def _mlp_kernel(x_ref, g_ref, gf_ref, wu_ref, wd_ref, o_ref, xn_ref, *, nj, final_norm):
    j = pl.program_id(1)

    @pl.when(j == 0)
    def _():
        x = x_ref[...]
        xn_ref[...] = _rms(x, g_ref[...]).astype(BF16)
        o_ref[...] = x

    h = _dot(xn_ref[...], wu_ref[...])
    h = jnp.square(jnp.maximum(h, 0.0)).astype(BF16)
    o_ref[...] += _dot(h, wd_ref[...])

    if final_norm:
        @pl.when(j == nj - 1)
        def _():
            o_ref[...] = _rms(o_ref[...], gf_ref[...])


def mlp(x, g, w_up, w_down, g_final=None):
    m, d = x.shape
    tm = _row_tile(m, 1024)
    tf = 1024
    nj = D_FF // tf
    final_norm = g_final is not None
    gf = g_final if final_norm else g
    return pl.pallas_call(
        functools.partial(_mlp_kernel, nj=nj, final_norm=final_norm),
        out_shape=jax.ShapeDtypeStruct((m, d), F32),
        grid=(m // tm, nj),
        in_specs=[
            pl.BlockSpec((tm, d), lambda i, j: (i, 0)),
            pl.BlockSpec((1, d), lambda i, j: (0, 0)),
            pl.BlockSpec((1, d), lambda i, j: (0, 0)),
            pl.BlockSpec((d, tf), lambda i, j: (0, j)),
            pl.BlockSpec((tf, d), lambda i, j: (j, 0)),
        ],
        out_specs=pl.BlockSpec((tm, d), lambda i, j: (i, 0)),
        scratch_shapes=[pltpu.VMEM((tm, d), BF16)],
        compiler_params=_params(("parallel", "arbitrary")),
        name="mlp",
    )(x, g.reshape(1, d), gf.reshape(1, d), w_up, w_down)


def _proj_res_kernel(a_ref, x_ref, w_ref, o_ref):
    o_ref[...] = x_ref[...] + _dot(a_ref[...].astype(BF16), w_ref[...])


def proj_res(a, x, w):
    m, d = x.shape
    tm = _row_tile(m, 512)
    return pl.pallas_call(
        _proj_res_kernel,
        out_shape=jax.ShapeDtypeStruct((m, d), F32),
        grid=(m // tm,),
        in_specs=[_row_spec(tm, a.shape[1]), _row_spec(tm, d), _full_spec(w.shape)],
        out_specs=_row_spec(tm, d),
        compiler_params=_params(("parallel",)),
        name="proj_res",
    )(a, x, w)


def _conv_in_kernel(x_ref, g_ref, wa_ref, wb_ref, o_ref):
    xn = _rms(x_ref[...], g_ref[...]).astype(BF16)
    o_ref[...] = _dot(xn, wa_ref[...]) * jax.nn.sigmoid(_dot(xn, wb_ref[...]))


def conv_in(x, g, w_a, w_b):
    m, d = x.shape
    tm = _row_tile(m, 512)
    return pl.pallas_call(
        _conv_in_kernel,
        out_shape=jax.ShapeDtypeStruct((m, d), F32),
        grid=(m // tm,),
        in_specs=[_row_spec(tm, d), _full_spec((1, d)), _full_spec(w_a.shape), _full_spec(w_b.shape)],
        out_specs=_row_spec(tm, d),
        compiler_params=_params(("parallel",)),
        name="conv_in",
    )(x, g.reshape(1, d), w_a, w_b)


def _ln_silu_proj(h, lng, lnb, w):
    mu = jnp.mean(h, axis=-1, keepdims=True)
    hc = h - mu
    var = jnp.mean(hc * hc, axis=-1, keepdims=True)
    hn = hc * lax.rsqrt(var + LN_EPS) * lng + lnb
    hn = hn * jax.nn.sigmoid(hn)
    return _dot(hn.astype(BF16), w)


CONV_HALO = 32
CONV_ROWS = 16


def _conv_seq_kernel(u_ref, halo_ref, x_ref, dw_ref, dwb_ref, lng_ref, lnb_ref, w_ref, o_ref,
                     up_ref, h_ref, *, tm):
    i = pl.program_id(1)
    up_ref[0, 0:CONV_HALO, :] = jnp.where(i > 0, halo_ref[0], 0.0)
    up_ref[0, CONV_HALO:, :] = u_ref[0]
    n_sh = tm + CONV_HALO - SUBLANES
    for s in range(1, SUBLANES):
        up_ref[s, 0:n_sh, :] = up_ref[0, s:s + n_sh, :]
    lead = CONV_HALO - (CONV_WIDTH - 1)
    for c in range(tm // CONV_ROWS):
        r0 = c * CONV_ROWS
        acc = jnp.zeros((CONV_ROWS, D_MODEL), F32)
        for j in range(CONV_WIDTH):
            s, a0 = (lead + j) % SUBLANES, r0 + (lead + j) // SUBLANES * SUBLANES
            acc = acc + up_ref[s, a0:a0 + CONV_ROWS, :] * dw_ref[j:j + 1, :]
        h_ref[r0:r0 + CONV_ROWS, :] = acc
    h = h_ref[...] + dwb_ref[...]
    o_ref[0] = x_ref[0] + _ln_silu_proj(h, lng_ref[...], lnb_ref[...], w_ref[...])


def conv_seq(u, x, dw, dwb, lng, lnb, w_out):
    b, s, d = u.shape
    tm = 128
    nt = s // tm
    hb = tm // CONV_HALO
    vec = lambda v: v.reshape(1, d)
    return pl.pallas_call(
        functools.partial(_conv_seq_kernel, tm=tm),
        out_shape=jax.ShapeDtypeStruct((b, s, d), F32),
        grid=(b, nt),
        in_specs=[
            pl.BlockSpec((1, tm, d), lambda bi, i: (bi, i, 0)),
            pl.BlockSpec((1, CONV_HALO, d), lambda bi, i: (bi, jnp.maximum(i * hb - 1, 0), 0)),
            pl.BlockSpec((1, tm, d), lambda bi, i: (bi, i, 0)),
            pl.BlockSpec((CONV_WIDTH, d), lambda bi, i: (0, 0)),
            pl.BlockSpec((1, d), lambda bi, i: (0, 0)),
            pl.BlockSpec((1, d), lambda bi, i: (0, 0)),
            pl.BlockSpec((1, d), lambda bi, i: (0, 0)),
            pl.BlockSpec((d, d), lambda bi, i: (0, 0)),
        ],
        out_specs=pl.BlockSpec((1, tm, d), lambda bi, i: (bi, i, 0)),
        scratch_shapes=[pltpu.VMEM((SUBLANES, tm + CONV_HALO, d), F32), pltpu.VMEM((tm, d), F32)],
        compiler_params=_params(("parallel", "arbitrary")),
        name="conv_seq",
    )(u, u, x, dw, vec(dwb), vec(lng), vec(lnb), w_out)


def _conv_step_kernel(u_ref, buf_ref, x_ref, dw_ref, dwb_ref, lng_ref, lnb_ref, w_ref, o_ref):
    nb = CONV_WIDTH - 1
    h = jnp.sum(buf_ref[...] * dw_ref[0:nb, :][None], axis=1)
    h = h + u_ref[...] * dw_ref[nb:nb + 1, :] + dwb_ref[...]
    o_ref[...] = x_ref[...] + _ln_silu_proj(h, lng_ref[...], lnb_ref[...], w_ref[...])


def conv_step(u, buf, x, dw, dwb, lng, lnb, w_out):
    b, d = u.shape
    tb = 32
    vec = lambda v: v.reshape(1, d)
    return pl.pallas_call(
        _conv_step_kernel,
        out_shape=jax.ShapeDtypeStruct((b, d), F32),
        grid=(b // tb,),
        in_specs=[
            _row_spec(tb, d),
            pl.BlockSpec((tb, CONV_WIDTH - 1, d), lambda i: (i, 0, 0)),
            _row_spec(tb, d),
            _full_spec((CONV_WIDTH, d)), _full_spec((1, d)), _full_spec((1, d)), _full_spec((1, d)),
            _full_spec((d, d)),
        ],
        out_specs=_row_spec(tb, d),
        compiler_params=_params(("parallel",)),
        name="conv_step",
    )(u, buf, x, dw, vec(dwb), vec(lng), vec(lnb), w_out)


POOL_HALO = 16


def _pool_project(d, w_ref, scale):
    ys = []
    for gi in range(len(POOL_WINDOWS)):
        dg = d[:, gi * POOL_GROUP:(gi + 1) * POOL_GROUP].astype(BF16)
        ys.append(_dot(dg, w_ref[gi]))
    return jnp.concatenate(ys, axis=-1) * scale


def _pool_seq_kernel(x_ref, halo_ref, g_ref, w_ref, sc_ref, o_ref, tail_ref, xc_ref, *, tm, nt):
    i = pl.program_id(1)
    g = g_ref[...]
    xn = _rms(x_ref[0], g)
    xc_ref[0:POOL_HALO, :] = jnp.where(i > 0, _rms(halo_ref[0], g), 0.0)
    xc_ref[POOL_HALO:, :] = xn
    pos = i * tm + lax.broadcasted_iota(jnp.int32, (tm, 1), 0)
    means = []
    for gi, w in enumerate(POOL_WINDOWS):
        c = slice(gi * POOL_GROUP, (gi + 1) * POOL_GROUP)
        win = xn[:, c]
        for k in range(1, w):
            win = win + xc_ref[POOL_HALO - k:POOL_HALO - k + tm, c]
        cnt = jnp.minimum(w, pos + 1).astype(F32)
        means.append(win / cnt)
    d = jnp.concatenate(means, axis=-1) - xn
    o_ref[0] = x_ref[0] + _pool_project(d, w_ref, sc_ref[...])

    @pl.when(i == nt - 1)
    def _():
        tail_ref[0] = xn[tm - POOL_HALO:, :]


def pool_seq(x, g, w_grp, scale):
    b, s, d = x.shape
    tm = 256
    nt = s // tm
    hb = tm // POOL_HALO
    return pl.pallas_call(
        functools.partial(_pool_seq_kernel, tm=tm, nt=nt),
        out_shape=(jax.ShapeDtypeStruct((b, s, d), F32), jax.ShapeDtypeStruct((b, POOL_HALO, d), F32)),
        grid=(b, nt),
        in_specs=[
            pl.BlockSpec((1, tm, d), lambda bi, i: (bi, i, 0)),
            pl.BlockSpec((1, POOL_HALO, d), lambda bi, i: (bi, jnp.maximum(i * hb - 1, 0), 0)),
            pl.BlockSpec((1, d), lambda bi, i: (0, 0)),
            pl.BlockSpec(w_grp.shape, lambda bi, i: (0, 0, 0)),
            pl.BlockSpec((1, d), lambda bi, i: (0, 0)),
        ],
        out_specs=(pl.BlockSpec((1, tm, d), lambda bi, i: (bi, i, 0)),
                   pl.BlockSpec((1, POOL_HALO, d), lambda bi, i: (bi, 0, 0))),
        scratch_shapes=[pltpu.VMEM((tm + POOL_HALO, d), F32)],
        compiler_params=_params(("parallel", "arbitrary")),
        name="pool_seq",
    )(x, x, g.reshape(1, d), w_grp, scale.reshape(1, d))


def _pool_step_kernel(x_ref, buf_ref, g_ref, msk_ref, icnt_ref, w_ref, sc_ref, o_ref, xn_ref):
    xn = _rms(x_ref[...], g_ref[...])
    win = xn + jnp.sum(buf_ref[...] * msk_ref[...][None], axis=1)
    d = win * icnt_ref[...] - xn
    o_ref[...] = x_ref[...] + _pool_project(d, w_ref, sc_ref[...])
    xn_ref[...] = xn


def pool_step(x, buf, start_pos, g, w_grp, scale):
    b, d = x.shape
    tb = 32
    rows = jnp.arange(POOL_BUF)[:, None]
    win_of_lane = jnp.repeat(jnp.asarray(POOL_WINDOWS), POOL_GROUP)[None, :]
    msk = (rows >= POOL_BUF - (win_of_lane - 1)).astype(F32)
    icnt = 1.0 / jnp.minimum(win_of_lane, start_pos + 1).astype(F32)
    return pl.pallas_call(
        _pool_step_kernel,
        out_shape=(jax.ShapeDtypeStruct((b, d), F32), jax.ShapeDtypeStruct((b, d), F32)),
        grid=(b // tb,),
        in_specs=[
            _row_spec(tb, d),
            pl.BlockSpec((tb, POOL_BUF, d), lambda i: (i, 0, 0)),
            _full_spec((1, d)), _full_spec((POOL_BUF, d)), _full_spec((1, d)),
            _full_spec(w_grp.shape), _full_spec((1, d)),
        ],
        out_specs=(_row_spec(tb, d), _row_spec(tb, d)),
        compiler_params=_params(("parallel",)),
        name="pool_step",
    )(x, buf, g.reshape(1, d), msk, icnt, w_grp, scale.reshape(1, d))


def _qkv_kernel(x_ref, g_ref, wq_ref, wk_ref, wv_ref, q_ref, k_ref, v_ref, kb_ref, vb_ref):
    xn = _rms(x_ref[...], g_ref[...]).astype(BF16)
    q_ref[...] = (_dot(xn, wq_ref[...]) * (SB_SCALE * LOG2E)).astype(BF16)
    k = _dot(xn, wk_ref[...])
    v = _dot(xn, wv_ref[...])
    k_ref[...] = k
    v_ref[...] = v
    kb_ref[...] = k.astype(BF16)
    vb_ref[...] = v.astype(BF16)


def qkv_proj(x, g, wq, wk, wv):
    m, d = x.shape
    tm = _row_tile(m, 512)
    sd = lambda dt: jax.ShapeDtypeStruct((m, d), dt)
    return pl.pallas_call(
        _qkv_kernel,
        out_shape=(sd(BF16), sd(F32), sd(F32), sd(BF16), sd(BF16)),
        grid=(m // tm,),
        in_specs=[_row_spec(tm, d), _full_spec((1, d)), _full_spec((d, d)), _full_spec((d, d)),
                  _full_spec((d, d))],
        out_specs=tuple(_row_spec(tm, d) for _ in range(5)),
        compiler_params=_params(("parallel",)),
        name="qkv_proj",
    )(x, g.reshape(1, d), wq, wk, wv)


def _qkv_seq_kernel(x_ref, g_ref, wq_ref, wkt_ref, wvt_ref, q_ref, kt_ref, vt_ref, ktb_ref, vtb_ref):
    xn = _rms(x_ref[0], g_ref[...]).astype(BF16)
    q_ref[0] = (_dot(xn, wq_ref[...]) * (SB_SCALE * LOG2E)).astype(BF16)
    kt = _dot_nt(wkt_ref[...], xn)
    vt = _dot_nt(wvt_ref[...], xn)
    kt_ref[0] = kt
    vt_ref[0] = vt
    ktb_ref[0] = kt.astype(BF16)
    vtb_ref[0] = vt.astype(BF16)


def qkv_proj_seq(x, g, wq, wk_t, wv_t):
    b, s, d = x.shape
    tm = 512
    rows = pl.BlockSpec((1, tm, d), lambda bi, i: (bi, i, 0))
    cols = pl.BlockSpec((1, d, tm), lambda bi, i: (bi, 0, i))
    full = lambda shape: pl.BlockSpec(shape, lambda bi, i: (0, 0))
    tsd = lambda dt: jax.ShapeDtypeStruct((b, d, s), dt)
    return pl.pallas_call(
        _qkv_seq_kernel,
        out_shape=(jax.ShapeDtypeStruct((b, s, d), BF16), tsd(F32), tsd(F32), tsd(BF16), tsd(BF16)),
        grid=(b, s // tm),
        in_specs=[rows, full((1, d)), full((d, d)), full((d, d)), full((d, d))],
        out_specs=(rows, cols, cols, cols, cols),
        compiler_params=_params(("parallel", "parallel")),
        name="qkv_proj_seq",
    )(x, g.reshape(1, d), wq, wk_t, wv_t)


def _softplus(z):
    return jnp.maximum(z, 0.0) + jnp.log(1.0 + jnp.exp(-jnp.abs(z)))


ATT_BLK = 256
ATT_Q = 512
ATT_ROWS = 128
ATT_GROUP = 4
LOG2E = 1.4426950408889634


def _softplus2(z):
    neg_abs = lax.bitcast_convert_type(
        lax.bitcast_convert_type(z, jnp.int32) | jnp.int32(-2 ** 31), F32)
    return jnp.maximum(z, 0.0) + jnp.log2(1.0 + jnp.exp2(neg_abs))


def _sb_prompt_kernel(bias_ref, q_ref, k_ref, v_ref, o_ref, z_scr, x_scr, p_scr):
    hp = pl.program_id(1)
    i = pl.program_id(2)
    t = ATT_BLK
    nd = ATT_Q // t
    nr = ATT_Q // ATT_ROWS
    row = lax.broadcasted_iota(jnp.int32, (t, t), 0)
    col = lax.broadcasted_iota(jnp.int32, (t, t), 1)
    neg_upper = jnp.where(row > col, -1.0, 0.0).astype(BF16)
    q_off = lax.broadcasted_iota(jnp.int32, (ATT_ROWS, t), 0)
    k_off = lax.broadcasted_iota(jnp.int32, (ATT_ROWS, t), 1)
    heads = [slice(hh * HEAD_DIM, (hh + 1) * HEAD_DIM) for hh in range(2)]
    rows = [slice(r * ATT_ROWS, (r + 1) * ATT_ROWS) for r in range(nr)]
    chains = [(hh, r) for hh in range(2) for r in range(nr)]
    nc = len(chains)
    bias = [bias_ref[2 * hp + hh] * LOG2E for hh in range(2)]
    q = [q_ref[0, rows[r], heads[hh]] for hh, r in chains]

    def logits(blk):
        kb = k_ref[0, :, pl.ds(pl.multiple_of(blk * t, t), t)]
        return [_dot(q[ci], kb[heads[hh], :]) + bias[hh] for ci, (hh, r) in enumerate(chains)]

    def stash_logits(zs):
        for ci in range(nc):
            z_scr[ci] = zs[ci]

    def stash_weights(zs, cs, masks):
        cs_new = []
        for g0 in range(0, nc, ATT_GROUP):
            group = range(g0, g0 + ATT_GROUP)
            sps, sp0 = [], []
            for ci in group:
                r = chains[ci][1]
                sp = _softplus2(zs[ci])
                x_scr[ci] = zs[ci] - sp
                sp = sp if masks[r] is None else jnp.where(masks[r], sp, 0.0)
                sps.append(sp.astype(BF16))
                sp0.append(sp[:, 0:1])
            betweens = [_dot(sp, neg_upper) for sp in sps]
            for ci, bt, s0 in zip(group, betweens, sp0):
                r = chains[ci][1]
                p = jnp.exp2(x_scr[ci] + bt + cs[ci])
                p_scr[ci] = (p if masks[r] is None else jnp.where(masks[r], p, 0.0)).astype(BF16)
                cs_new.append(cs[ci] + bt[:, 0:1] - s0)
        return cs_new

    def weighted_values(blk, accs):
        vb = v_ref[0, :, pl.ds(pl.multiple_of(blk * t, t), t)]
        return [acc + _dot_nt(p_scr[ci], vb[heads[hh], :]) for ci, (acc, (hh, r)) in enumerate(zip(accs, chains))]

    cs = [jnp.zeros((ATT_ROWS, 1), F32)] * nc
    accs = [jnp.zeros((ATT_ROWS, HEAD_DIM), F32)] * nc
    zs = logits(nd * i + nd - 1)
    for dblk in reversed(range(nd)):
        blk = nd * i + dblk
        z_next = logits(jnp.maximum(blk - 1, 0))
        if dblk < nd - 1:
            accs = weighted_values(blk + 1, accs)
        masks = [(dblk * t + k_off) < (r * ATT_ROWS + q_off) for r in range(nr)]
        cs = stash_weights(zs, cs, masks)
        zs = z_next
    stash_logits(zs)

    def body(n, carry):
        cs, accs = carry
        blk = nd * i - 1 - n
        zs = [z_scr[ci] for ci in range(nc)]
        stash_logits(logits(jnp.maximum(blk - 1, 0)))
        accs = weighted_values(blk + 1, accs)
        cs = stash_weights(zs, cs, [None] * nr)
        return cs, accs

    cs, accs = lax.fori_loop(0, nd * i, body, (cs, accs))
    accs = weighted_values(0, accs)
    o_ref[0] = jnp.concatenate(
        [jnp.concatenate([accs[hh * nr + r] for r in range(nr)], axis=0) for hh in range(2)], axis=-1)


def sb_attn_prompt(q, k, v, bias):
    b, s, d = q.shape
    nc = 2 * ATT_Q // ATT_ROWS
    return pl.pallas_call(
        _sb_prompt_kernel,
        out_shape=jax.ShapeDtypeStruct((b, s, d), F32),
        grid=(b, N_HEADS // 2, s // ATT_Q),
        in_specs=[
            pl.BlockSpec(memory_space=pltpu.SMEM),
            pl.BlockSpec((1, ATT_Q, LANES), lambda bi, hp, i: (bi, i, hp)),
            pl.BlockSpec((1, LANES, s), lambda bi, hp, i: (bi, hp, 0)),
            pl.BlockSpec((1, LANES, s), lambda bi, hp, i: (bi, hp, 0)),
        ],
        out_specs=pl.BlockSpec((1, ATT_Q, LANES), lambda bi, hp, i: (bi, i, hp)),
        scratch_shapes=[pltpu.VMEM((nc, ATT_ROWS, ATT_BLK), F32), pltpu.VMEM((nc, ATT_ROWS, ATT_BLK), F32),
                        pltpu.VMEM((nc, ATT_ROWS, ATT_BLK), BF16)],
        compiler_params=_params(("parallel", "parallel", "arbitrary")),
        name="sb_attn_prompt",
    )(bias, q, k, v)


DEC_PAGES = 8


def _sb_decode_kernel(pt_ref, q_ref, kn_ref, vn_ref, bias_ref, *refs, n_steps):
    del pt_ref
    kp_refs, vp_refs = refs[:DEC_PAGES], refs[DEC_PAGES:2 * DEC_PAGES]
    o_ref, c_ref, qcol_ref, acc_ref, onew_ref = refs[2 * DEC_PAGES:]
    p = pl.program_id(1)
    t = PAGE_SIZE
    dh = HEAD_DIM
    bias = bias_ref[...] * LOG2E
    eye = (lax.broadcasted_iota(jnp.int32, (1, dh, dh), 1) == lax.broadcasted_iota(jnp.int32, (1, dh, dh), 2))

    @pl.when(p == 0)
    def _():
        q = q_ref[0]
        zn = jnp.sum(kn_ref[0] * q, axis=-1, keepdims=True) + bias
        q_pos = n_steps * DEC_PAGES * PAGE_SIZE
        visible = (q_pos + lax.broadcasted_iota(jnp.int32, zn.shape, 1)) < q_pos
        spn = _softplus2(zn)
        c_ref[...] = jnp.where(visible, -spn, 0.0)
        onew_ref[...] = jnp.where(visible, jnp.exp2(zn - spn), 0.0) * vn_ref[0]
        acc_ref[...] = jnp.zeros_like(acc_ref)
        q_col = jnp.sum(jnp.where(eye, q[:, None, :], 0.0), axis=-1, keepdims=True)
        qcol_ref[...] = jnp.broadcast_to(q_col, qcol_ref.shape)

    row = lax.broadcasted_iota(jnp.int32, (t, t), 0)
    col = lax.broadcasted_iota(jnp.int32, (t, t), 1)
    neg_upper = jnp.where(row > col, -1.0, 0.0).astype(BF16)
    c = c_ref[...]
    q_col = qcol_ref[...]
    for g in range(DEC_PAGES):
        z = jnp.sum(kp_refs[g][0] * q_col, axis=1) + bias
        sp = _softplus2(z)
        between = _dot(sp.astype(BF16), neg_upper)
        a = jnp.exp2(z - sp + between + c)
        acc_ref[...] += vp_refs[g][0] * a[:, None, :]
        c = c + between[:, 0:1] - sp[:, 0:1]
    c_ref[...] = c

    @pl.when(p == n_steps - 1)
    def _():
        o_col = jnp.sum(acc_ref[...], axis=-1, keepdims=True)
        o_ref[0] = jnp.sum(jnp.where(eye, o_col, 0.0), axis=1) + onew_ref[...]


def sb_attn_decode(q, k_new, v_new, bias, cache_k, cache_v, page_table):
    b = q.shape[0]
    n_pages = page_table.shape[1]
    n_steps = n_pages // DEC_PAGES
    hd = (N_HEADS, HEAD_DIM)

    def page(g):
        return lambda bi, p, pt: (pt[bi * n_pages + (n_pages - 1 - (p * DEC_PAGES + g))], 0, 0, 0)

    row_spec = pl.BlockSpec((1,) + hd, lambda bi, p, pt: (bi, 0, 0))
    page_specs = [pl.BlockSpec((1,) + hd + (PAGE_SIZE,), page(g)) for g in range(DEC_PAGES)]
    return pl.pallas_call(
        functools.partial(_sb_decode_kernel, n_steps=n_steps),
        out_shape=jax.ShapeDtypeStruct((b,) + hd, F32),
        grid_spec=pltpu.PrefetchScalarGridSpec(
            num_scalar_prefetch=1,
            grid=(b, n_steps),
            in_specs=[row_spec, row_spec, row_spec, pl.BlockSpec((N_HEADS, 1), lambda bi, p, pt: (0, 0))]
            + page_specs + page_specs,
            out_specs=row_spec,
            scratch_shapes=[pltpu.VMEM((N_HEADS, 1), F32), pltpu.VMEM(hd + (PAGE_SIZE,), F32),
                            pltpu.VMEM(hd + (PAGE_SIZE,), F32), pltpu.VMEM(hd, F32)],
        ),
        compiler_params=_params(("parallel", "arbitrary")),
        name="sb_attn_decode",
    )(page_table.reshape(-1), q, k_new, v_new, bias.reshape(N_HEADS, 1),
      *([cache_k] * DEC_PAGES), *([cache_v] * DEC_PAGES))


def _head_sum(x, sel, expand):
    return _dot_x2(_dot_x2(x, sel), expand)


def _rwkv_in_kernel(x_ref, prev_ref, g_ref, mix_ref, wr_ref, wk_ref, wv_ref, w0_ref, w1_ref, w2_ref,
                    a0_ref, a1_ref, a2_ref, g1_ref, g2_ref, kk_ref, ka_ref, sel_ref, exp_ref,
                    r_o, lw_o, k_o, v_o, kk_o, a_o, g_o, xn_o):
    xn = _rms(x_ref[...], g_ref[...])
    xx = prev_ref[...] - xn
    mixed = lambda c: (xn + xx * mix_ref[c:c + 1, :]).astype(BF16)
    xr, xw, xk, xv, xa, xg = (mixed(c) for c in range(6))
    r = _dot(xr, wr_ref[...])
    w = w0_ref[...] + _dot(jnp.tanh(_dot(xw, w1_ref[...])).astype(BF16), w2_ref[...])
    w = -_softplus(-w) - 0.5
    k = _dot(xk, wk_ref[...])
    v = _dot(xv, wv_ref[...])
    a = jax.nn.sigmoid(a0_ref[...] + _dot(_dot(xa, a1_ref[...]).astype(BF16), a2_ref[...]))
    g = _dot(jax.nn.sigmoid(_dot(xg, g1_ref[...])).astype(BF16), g2_ref[...])
    kk = k * kk_ref[...]
    nrm = jnp.sqrt(_head_sum(kk * kk, sel_ref[...], exp_ref[...]))
    kk = kk / jnp.maximum(nrm, 1e-12)
    r_o[...] = r
    lw_o[...] = -jnp.exp(w)
    k_o[...] = k * (1.0 + (a - 1.0) * ka_ref[...])
    v_o[...] = v
    kk_o[...] = kk
    a_o[...] = a
    g_o[...] = g
    xn_o[...] = xn


def _head_selectors():
    lane = jnp.arange(D_MODEL)[:, None] // HEAD_DIM
    sel = (lane == jnp.arange(LANES)[None, :]).astype(BF16)
    return sel, sel.T


def rwkv_in(x, x_prev_rows, g, p):
    m, d = x.shape
    tm = _row_tile(m, 256)
    sel, expand = _head_selectors()
    vec = lambda v: v.reshape(1, d)
    ws = [p["mix"], p["w_r"], p["w_k"], p["w_v"], vec(p["w0"]), p["w1"], p["w2"], vec(p["a0"]), p["a1"],
          p["a2"], p["g1"], p["g2"], vec(p["k_k"]), vec(p["k_a"]), sel, expand]
    sd = jax.ShapeDtypeStruct((m, d), F32)
    return pl.pallas_call(
        _rwkv_in_kernel,
        out_shape=(sd,) * 8,
        grid=(m // tm,),
        in_specs=[_row_spec(tm, d), _row_spec(tm, d), _full_spec((1, d))] + [_full_spec(w.shape) for w in ws],
        out_specs=tuple(_row_spec(tm, d) for _ in range(8)),
        compiler_params=_params(("parallel",)),
        name="rwkv_in",
    )(x, x_prev_rows, vec(g), *ws)


def _norm_shift_kernel(x_ref, halo_ref, g_ref, o_ref, *, tm):
    i = pl.program_id(1)
    g = g_ref[...]
    xn = _rms(x_ref[0], g)
    last = _rms(halo_ref[0], g)[7:8, :]
    first = jnp.where(i > 0, last, 0.0)
    rolled = pltpu.roll(xn, 1, axis=0)
    row = lax.broadcasted_iota(jnp.int32, (tm, 1), 0)
    o_ref[0] = jnp.where(row == 0, first, rolled)


def norm_shift(x, g):
    b, s, d = x.shape
    tm = 512
    hb = tm // 8
    return pl.pallas_call(
        functools.partial(_norm_shift_kernel, tm=tm),
        out_shape=jax.ShapeDtypeStruct((b, s, d), F32),
        grid=(b, s // tm),
        in_specs=[
            pl.BlockSpec((1, tm, d), lambda bi, i: (bi, i, 0)),
            pl.BlockSpec((1, 8, d), lambda bi, i: (bi, jnp.maximum(i * hb - 1, 0), 0)),
            pl.BlockSpec((1, d), lambda bi, i: (0, 0)),
        ],
        out_specs=pl.BlockSpec((1, tm, d), lambda bi, i: (bi, i, 0)),
        compiler_params=_params(("parallel", "arbitrary")),
        name="norm_shift",
    )(x, x, g.reshape(1, d))


WKV_CHUNK = 64


def _wkv_seq_kernel(r_ref, lw_ref, k_ref, v_ref, kk_ref, a_ref, o_ref, s_ref, h_ref, *, tb, nt):
    i = pl.program_id(1)
    c = WKV_CHUNK
    dh = HEAD_DIM

    @pl.when(i == 0)
    def _():
        h_ref[...] = jnp.zeros_like(h_ref)

    row = lax.broadcasted_iota(jnp.int32, (c, c), 0)
    col = lax.broadcasted_iota(jnp.int32, (c, c), 1)
    tri_incl = _ones_where(col <= row)
    eye = row == col
    row2 = lax.broadcasted_iota(jnp.int32, (2 * c, 2 * c), 0)
    col2 = lax.broadcasted_iota(jnp.int32, (2 * c, 2 * c), 1)
    colm = jnp.where(col2 < c, col2, col2 - c)
    gmask = colm < jnp.where(row2 < c, row2, row2 - c + 1)

    def chunk(n, carry):
        t0 = pl.multiple_of(n * c, c)
        sl = pl.ds(t0, c)
        lw = lw_ref[0, sl, :]
        cl = _dot_x3(tri_incl, lw)
        kk = kk_ref[0, sl, :]
        cl_end = cl[c - 1:c, :]
        g_end = jnp.exp(cl_end)
        inv = jnp.exp(-cl)
        at = (-kk * jnp.exp(cl - lw)).astype(BF16)
        rt_f = r_ref[0, sl, :] * jnp.exp(cl)
        rt = rt_f.astype(BF16)
        b = kk * a_ref[0, sl, :]
        k = k_ref[0, sl, :]
        bt = (b * inv).astype(BF16)
        kt = (k * inv).astype(BF16)
        to_end = jnp.exp(cl_end - cl)
        be = (b * to_end).astype(BF16)
        ke = (k * to_end).astype(BF16)
        vb = v_ref[0, sl, :].astype(BF16)
        heads = [slice(h * dh, (h + 1) * dh) for h in range(N_HEADS)]
        gms = [jnp.where(gmask, _dot_nt(jnp.concatenate([at[:, ls], rt[:, ls]], axis=0),
                                        jnp.concatenate([bt[:, ls], kt[:, ls]], axis=0)), 0.0)
               for ls in heads]
        mabs = [gm[0:c, 0:c] for gm in gms]
        mvs = [_dot(gm[0:c, c:].astype(BF16), vb[:, ls]) for gm, ls in zip(gms, heads)]
        pws = [m.astype(BF16) for m in mabs]
        tms = [jnp.where(eye, 1.0, m) for m in mabs]
        for _ in range(5):
            pws = [_dot(pw, pw).astype(BF16) for pw in pws]
            tms = [tm_ + _dot(tm_.astype(BF16), pw) for tm_, pw in zip(tms, pws)]
        aws = [_dot(tm_.astype(BF16), jnp.concatenate([at[:, ls], mv.astype(BF16)], axis=1))
               for tm_, mv, ls in zip(tms, mvs, heads)]
        zmats = [jnp.concatenate([aw.astype(BF16),
                                  jnp.concatenate([jnp.zeros((c, dh), BF16), vb[:, ls]], axis=1)], axis=0)
                 for aw, ls in zip(aws, heads)]
        x1s = [_dot(gm[c:, :].astype(BF16), zm) for gm, zm in zip(gms, zmats)]
        x2s = [_dot_tn(jnp.concatenate([be[:, ls], ke[:, ls]], axis=0), zm)
               for zm, ls in zip(zmats, heads)]
        ress = []
        for h, (x1, x2, ls) in enumerate(zip(x1s, x2s, heads)):
            dmat = jnp.where(eye, jnp.broadcast_to(g_end[:, ls], (c, dh)), 0.0)
            lb = jnp.concatenate([rt_f[:, ls] + x1[:, 0:dh], dmat + x2[:, 0:dh]], axis=0).astype(BF16)
            hi, lo = _split2(h_ref[h])
            ress.append(_dot(lb, hi) + _dot(lb, lo))
        for h, (res, x2) in enumerate(zip(ress, x2s)):
            h_ref[h] = res[c:, :] + x2[:, dh:]
        o_ref[0, sl, :] = jnp.concatenate([res[0:c, :] + x1[:, dh:] for res, x1 in zip(ress, x1s)], axis=-1)
        return carry

    lax.fori_loop(0, tb // c, chunk, 0)

    @pl.when(i == nt - 1)
    def _():
        s_ref[0] = h_ref[...]


def wkv_seq(r, lw, k, v, kk, a):
    b, s, d = r.shape
    tb = 256
    nt = s // tb
    blk = pl.BlockSpec((1, tb, d), lambda bi, i: (bi, i, 0))
    return pl.pallas_call(
        functools.partial(_wkv_seq_kernel, tb=tb, nt=nt),
        out_shape=(jax.ShapeDtypeStruct((b, s, d), F32),
                   jax.ShapeDtypeStruct((b, N_HEADS, HEAD_DIM, HEAD_DIM), F32)),
        grid=(b, nt),
        in_specs=[blk] * 6,
        out_specs=(blk, pl.BlockSpec((1, N_HEADS, HEAD_DIM, HEAD_DIM), lambda bi, i: (bi, 0, 0, 0))),
        scratch_shapes=[pltpu.VMEM((N_HEADS, HEAD_DIM, HEAD_DIM), F32)],
        compiler_params=_params(("parallel", "arbitrary")),
        name="wkv_seq",
    )(r, lw, k, v, kk, a)


def _wkv_step_kernel(r_ref, lw_ref, k_ref, v_ref, kk_ref, a_ref, s0_ref, o_ref, s_ref, *, tb):
    dh = HEAD_DIM
    eye = (lax.broadcasted_iota(jnp.int32, (1, dh, dh), 1) == lax.broadcasted_iota(jnp.int32, (1, dh, dh), 2))
    bs = range(tb)
    along_keys = lambda ref, bi: ref[bi][:, None, :]
    s0 = [s0_ref[bi] for bi in bs]
    kk = [along_keys(kk_ref, bi) for bi in bs]
    v_col = [jnp.sum(jnp.where(eye, along_keys(v_ref, bi), 0.0), axis=-1, keepdims=True) for bi in bs]
    sa = [jnp.sum(s0[bi] * -kk[bi], axis=-1, keepdims=True) for bi in bs]
    s1 = [s0[bi] * jnp.exp(along_keys(lw_ref, bi)) + sa[bi] * (kk[bi] * along_keys(a_ref, bi))
          + v_col[bi] * along_keys(k_ref, bi) for bi in bs]
    for bi in bs:
        s_ref[bi] = s1[bi]
    o_col = [jnp.sum(s1[bi] * along_keys(r_ref, bi), axis=-1, keepdims=True) for bi in bs]
    for bi in bs:
        o_ref[bi] = jnp.sum(jnp.where(eye, o_col[bi], 0.0), axis=1)


def wkv_step(r, lw, k, v, kk, a, s0):
    b = r.shape[0]
    tb = 8
    hd = (N_HEADS, HEAD_DIM)
    rows = pl.BlockSpec((tb,) + hd, lambda i: (i, 0, 0))
    st = pl.BlockSpec((tb,) + hd + (HEAD_DIM,), lambda i: (i, 0, 0, 0))
    return pl.pallas_call(
        functools.partial(_wkv_step_kernel, tb=tb),
        out_shape=(jax.ShapeDtypeStruct((b,) + hd, F32), jax.ShapeDtypeStruct(s0.shape, F32)),
        grid=(b // tb,),
        in_specs=[rows] * 6 + [st],
        out_specs=(rows, st),
        compiler_params=_params(("parallel",)),
        name="wkv_step",
    )(r, lw, k, v, kk, a, s0)


def _rwkv_out_kernel(o_ref, r_ref, k_ref, v_ref, g_ref, x_ref, rk_ref, lng_ref, lnb_ref, sel_ref, exp_ref,
                     w_ref, y_ref):
    sel, expand = sel_ref[...], exp_ref[...]
    o = o_ref[...]
    mu = _head_sum(o, sel, expand) * (1.0 / HEAD_DIM)
    oc = o - mu
    var = _head_sum(oc * oc, sel, expand) * (1.0 / HEAD_DIM)
    on = oc * lax.rsqrt(var + GN_EPS) * lng_ref[...] + lnb_ref[...]
    bonus = _head_sum(r_ref[...] * k_ref[...] * rk_ref[...], sel, expand)
    on = on + bonus * v_ref[...]
    y_ref[...] = x_ref[...] + _dot((on * g_ref[...]).astype(BF16), w_ref[...])


def rwkv_out(o, r, k, v, g, x, r_k, ln_g, ln_b, w_o):
    m, d = x.shape
    tm = _row_tile(m, 256)
    sel, expand = _head_selectors()
    vec = lambda t: t.reshape(1, d)
    return pl.pallas_call(
        _rwkv_out_kernel,
        out_shape=jax.ShapeDtypeStruct((m, d), F32),
        grid=(m // tm,),
        in_specs=[_row_spec(tm, d)] * 6 + [_full_spec((1, d))] * 3
        + [_full_spec(sel.shape), _full_spec(expand.shape), _full_spec((d, d))],
        out_specs=_row_spec(tm, d),
        compiler_params=_params(("parallel",)),
        name="rwkv_out",
    )(o, r, k, v, g, x, vec(r_k), vec(ln_g), vec(ln_b), sel, expand, w_o)


def _trunk(x, seq, start_pos, conv_buf, pool_buf, kv_past, wkv0, shift0, w):
    b, t, d = x.shape
    m = b * t
    flat = lambda z: z.reshape(m, d)
    x2 = flat(x)

    u = conv_in(x2, w["norm_mix"][0], w["conv_wa"], w["conv_wb"])
    cw = (w["conv_dw"], w["conv_dw_b"], w["conv_ln_g"], w["conv_ln_b"], w["conv_w_out"])
    if seq:
        x2 = flat(conv_seq(u.reshape(b, t, d), x, *cw))
        new_conv = u.reshape(b, t, d)[:, t - (CONV_WIDTH - 1):]
    else:
        x2 = conv_step(u, conv_buf, x2, *cw)
        new_conv = jnp.concatenate([conv_buf[:, 1:], u[:, None]], axis=1)
    x2 = mlp(x2, w["norm_mlp"][0], w["mlp_up"][0], w["mlp_down"][0])

    if seq:
        y, tail = pool_seq(x2.reshape(b, t, d), w["norm_mix"][1], w["pool_w"], w["pool_scale"])
        x2 = flat(y)
        new_pool = tail[:, POOL_HALO - POOL_BUF:]
    else:
        x2, xn = pool_step(x2, pool_buf, start_pos, w["norm_mix"][1], w["pool_w"], w["pool_scale"])
        new_pool = jnp.concatenate([pool_buf[:, 1:], xn[:, None]], axis=1)
    x2 = mlp(x2, w["norm_mlp"][1], w["mlp_up"][1], w["mlp_down"][1])

    if seq:
        q, kt, vt, ktb, vtb = qkv_proj_seq(x2.reshape(b, t, d), w["norm_mix"][2], w["attn_wq"],
                                           w["attn_wk_t"], w["attn_wv_t"])
        att = flat(sb_attn_prompt(q, ktb, vtb, w["attn_sb_bias"]))
        new_k = jnp.transpose(kt.reshape(b, N_HEADS, HEAD_DIM, t), (0, 3, 1, 2))
        new_v = jnp.transpose(vt.reshape(b, N_HEADS, HEAD_DIM, t), (0, 3, 1, 2))
    else:
        q, k, v, _, _ = qkv_proj(x2, w["norm_mix"][2], w["attn_wq"], w["attn_wk"], w["attn_wv"])
        cache_k, cache_v, page_table = kv_past
        hd = lambda z: z.astype(F32).reshape(b, N_HEADS, HEAD_DIM)
        att = sb_attn_decode(hd(q), hd(k), hd(v), w["attn_sb_bias"], cache_k, cache_v, page_table)
        att = att.reshape(b, d)
        new_k = k.reshape(b, t, N_HEADS, HEAD_DIM)
        new_v = v.reshape(b, t, N_HEADS, HEAD_DIM)
    x2 = proj_res(att, x2, w["attn_w_o"])
    x2 = mlp(x2, w["norm_mlp"][2], w["mlp_up"][2], w["mlp_down"][2])

    if seq:
        prev = flat(norm_shift(x2.reshape(b, t, d), w["norm_mix"][3]))
    else:
        prev = shift0
    r, lw, k2, v2, kk, a, g, xn = rwkv_in(x2, prev, w["norm_mix"][3], w["rw"])
    if seq:
        sh = lambda z: z.reshape(b, t, d)
        o, hstate = wkv_seq(sh(r), sh(lw), sh(k2), sh(v2), sh(kk), sh(a))
        o = flat(o)
        new_wkv = jnp.swapaxes(hstate, -1, -2)
        new_shift = xn.reshape(b, t, d)[:, t - 1]
    else:
        hd = lambda z: z.reshape(b, N_HEADS, HEAD_DIM)
        o, new_wkv = wkv_step(hd(r), hd(lw), hd(k2), hd(v2), hd(kk), hd(a), wkv0)
        o = o.reshape(b, d)
        new_shift = xn
    x2 = rwkv_out(o, r, k2, v2, g, x2, w["rw"]["r_k"], w["rw"]["ln_g"], w["rw"]["ln_b"], w["rw"]["w_o"])
    x2 = mlp(x2, w["norm_mlp"][3], w["mlp_up"][3], w["mlp_down"][3], g_final=w["norm_final"])

    return x2.reshape(b, t, d), new_conv, new_pool, new_k, new_v, new_wkv, new_shift


def kernel(x_prompt, x_sample, cache_conv, cache_pool, cache_k, cache_v, page_table, state_wkv, state_shift,
           norm_mix, norm_mlp, norm_final, mlp_w_up, mlp_w_down,
           conv_w_in, conv_dw, conv_dw_b, conv_ln_g, conv_ln_b, conv_w_out,
           pool_w, pool_scale, attn_w_qkv, attn_w_o, attn_sb_bias,
           rw_mix, rw_w_r, rw_w_k, rw_w_v, rw_w_o, rw_w0, rw_w1, rw_w2, rw_a0, rw_a1, rw_a2,
           rw_g1, rw_g2, rw_k_k, rw_k_a, rw_r_k, rw_ln_g, rw_ln_b):
    d = D_MODEL
    bf = lambda z: z.astype(BF16)
    gl = rw_g1.shape[-1]
    glp = -(-gl // LANES) * LANES
    g1 = jnp.pad(rw_g1[0], ((0, 0), (0, glp - gl)))
    g2 = jnp.pad(rw_g2[0], ((0, glp - gl), (0, 0)))
    w = dict(
        norm_mix=norm_mix, norm_mlp=norm_mlp, norm_final=norm_final,
        mlp_up=bf(mlp_w_up), mlp_down=bf(mlp_w_down),
        conv_wa=bf(conv_w_in[0, :, :d]), conv_wb=bf(conv_w_in[0, :, d:]),
        conv_dw=conv_dw[0], conv_dw_b=conv_dw_b[0], conv_ln_g=conv_ln_g[0], conv_ln_b=conv_ln_b[0],
        conv_w_out=bf(conv_w_out[0]),
        pool_w=bf(pool_w[0]), pool_scale=pool_scale[0],
        attn_wq=bf(attn_w_qkv[0, :, :d]), attn_wk=bf(attn_w_qkv[0, :, d:2 * d]),
        attn_wv=bf(attn_w_qkv[0, :, 2 * d:]), attn_w_o=bf(attn_w_o[0]), attn_sb_bias=attn_sb_bias[0],
        attn_wk_t=bf(attn_w_qkv[0, :, d:2 * d].T), attn_wv_t=bf(attn_w_qkv[0, :, 2 * d:].T),
        rw=dict(mix=rw_mix[0], w_r=bf(rw_w_r[0]), w_k=bf(rw_w_k[0]), w_v=bf(rw_w_v[0]), w_o=bf(rw_w_o[0]),
                w0=rw_w0[0], w1=bf(rw_w1[0]), w2=bf(rw_w2[0]), a0=rw_a0[0], a1=bf(rw_a1[0]), a2=bf(rw_a2[0]),
                g1=bf(g1), g2=bf(g2), k_k=rw_k_k[0], k_a=rw_k_a[0], r_k=rw_r_k[0].reshape(-1),
                ln_g=rw_ln_g[0], ln_b=rw_ln_b[0]),
    )
    yp, conv_p, pool_p, k_p, v_p, wkv_p, shift_p = _trunk(
        x_prompt, True, 0, None, None, None, None, None, w)
    past_len = page_table.shape[1] * PAGE_SIZE
    ys, conv_s, pool_s, k_s, v_s, wkv_s, shift_s = _trunk(
        x_sample, False, past_len, cache_conv[0], cache_pool[0],
        (jnp.transpose(cache_k[0], (0, 2, 3, 1)), jnp.transpose(cache_v[0], (0, 2, 3, 1)), page_table),
        state_wkv[0], state_shift[0], w)
    st = lambda z: z[None]
    return (yp, ys, st(conv_p), st(conv_s), st(pool_p), st(pool_s), st(k_p), st(v_p), st(k_s), st(v_s),
            st(wkv_p), st(wkv_s), st(shift_p), st(shift_s))
```

```python
import functools

import jax
import jax.numpy as jnp
from jax import lax
from jax.experimental import pallas as pl
from jax.experimental.pallas import tpu as pltpu

F32 = jnp.float32
BF16 = jnp.bfloat16

D_MODEL = 1024
D_FF = 4 * D_MODEL
HEAD_DIM = 64
N_HEADS = D_MODEL // HEAD_DIM
CONV_WIDTH = 31
POOL_WINDOWS = (2, 4, 8, 16)
POOL_GROUP = D_MODEL // len(POOL_WINDOWS)
POOL_BUF = max(POOL_WINDOWS) - 1
PAGE_SIZE = 128
RMS_EPS = 1e-6
LN_EPS = 1e-5
GN_EPS = 1e-5 * HEAD_DIM
SB_SCALE = HEAD_DIM ** -0.5

LANES = 128
SUBLANES = 8
MXU_DIM = 256
VMEM_LIMIT = 48 << 20


def _params(sem, vmem=VMEM_LIMIT):
    return pltpu.CompilerParams(dimension_semantics=sem, vmem_limit_bytes=vmem)


def _rms(x, g):
    return x * lax.rsqrt(jnp.mean(x * x, axis=-1, keepdims=True) + RMS_EPS) * g


def _dot(a, b):
    return jnp.dot(a, b, preferred_element_type=F32)


def _dot_nt(a, b):
    return lax.dot_general(a, b, (((1,), (1,)), ((), ())), preferred_element_type=F32)


def _dot_tn(a, b):
    return lax.dot_general(a, b, (((0,), (0,)), ((), ())), preferred_element_type=F32)


def _ones_where(cond):
    return jnp.where(cond, 1.0, 0.0).astype(BF16)


def _split2(x):
    hi = x.astype(BF16)
    lo = (x - hi.astype(F32)).astype(BF16)
    return hi, lo


def _dot_x2(x, w):
    hi, lo = _split2(x)
    return _dot(hi, w) + _dot(lo, w)


def _dot_x3(w, x):
    hi = x.astype(BF16)
    r1 = x - hi.astype(F32)
    mid = r1.astype(BF16)
    lo = (r1 - mid.astype(F32)).astype(BF16)
    return _dot(w, hi) + _dot(w, mid) + _dot(w, lo)


def _row_spec(tm, n):
    return pl.BlockSpec((tm, n), lambda i: (i, 0))


def _full_spec(shape):
    return pl.BlockSpec(shape, lambda *_: (0,) * len(shape))


def _row_tile(m, pref):
    return pref if m % pref == 0 else m


def _mlp_kernel(x_ref, g_ref, gf_ref, wu_ref, wd_ref, o_ref, xn_ref, *, nj, final_norm):
    j = pl.program_id(1)

    @pl.when(j == 0)
    def _():
        x = x_ref[...]
        xn_ref[...] = _rms(x, g_ref[...]).astype(BF16)
        o_ref[...] = x

    h = _dot(xn_ref[...], wu_ref[...])
    h = jnp.square(jnp.maximum(h, 0.0)).astype(BF16)
    o_ref[...] += _dot(h, wd_ref[...])

    if final_norm:
        @pl.when(j == nj - 1)
        def _():
            o_ref[...] = _rms(o_ref[...], gf_ref[...])


def mlp(x, g, w_up, w_down, g_final=None):
    m, d = x.shape
    tm = _row_tile(m, 1024)
    tf = 1024
    nj = D_FF // tf
    final_norm = g_final is not None
    gf = g_final if final_norm else g
    return pl.pallas_call(
        functools.partial(_mlp_kernel, nj=nj, final_norm=final_norm),
        out_shape=jax.ShapeDtypeStruct((m, d), F32),
        grid=(m // tm, nj),
        in_specs=[
            pl.BlockSpec((tm, d), lambda i, j: (i, 0)),
            pl.BlockSpec((1, d), lambda i, j: (0, 0)),
            pl.BlockSpec((1, d), lambda i, j: (0, 0)),
            pl.BlockSpec((d, tf), lambda i, j: (0, j)),
            pl.BlockSpec((tf, d), lambda i, j: (j, 0)),
        ],
        out_specs=pl.BlockSpec((tm, d), lambda i, j: (i, 0)),
        scratch_shapes=[pltpu.VMEM((tm, d), BF16)],
        compiler_params=_params(("parallel", "arbitrary")),
        name="mlp",
    )(x, g.reshape(1, d), gf.reshape(1, d), w_up, w_down)


def _proj_res_kernel(a_ref, x_ref, w_ref, o_ref):
    o_ref[...] = x_ref[...] + _dot(a_ref[...].astype(BF16), w_ref[...])


def proj_res(a, x, w):
    m, d = x.shape
    tm = _row_tile(m, 512)
    return pl.pallas_call(
        _proj_res_kernel,
        out_shape=jax.ShapeDtypeStruct((m, d), F32),
        grid=(m // tm,),
        in_specs=[_row_spec(tm, a.shape[1]), _row_spec(tm, d), _full_spec(w.shape)],
        out_specs=_row_spec(tm, d),
        compiler_params=_params(("parallel",)),
        name="proj_res",
    )(a, x, w)


def _conv_in_kernel(x_ref, g_ref, wa_ref, wb_ref, o_ref):
    xn = _rms(x_ref[...], g_ref[...]).astype(BF16)
    o_ref[...] = _dot(xn, wa_ref[...]) * jax.nn.sigmoid(_dot(xn, wb_ref[...]))


def conv_in(x, g, w_a, w_b):
    m, d = x.shape
    tm = _row_tile(m, 512)
    return pl.pallas_call(
        _conv_in_kernel,
        out_shape=jax.ShapeDtypeStruct((m, d), F32),
        grid=(m // tm,),
        in_specs=[_row_spec(tm, d), _full_spec((1, d)), _full_spec(w_a.shape), _full_spec(w_b.shape)],
        out_specs=_row_spec(tm, d),
        compiler_params=_params(("parallel",)),
        name="conv_in",
    )(x, g.reshape(1, d), w_a, w_b)


def _ln_silu_proj(h, lng, lnb, w):
    mu = jnp.mean(h, axis=-1, keepdims=True)
    hc = h - mu
    var = jnp.mean(hc * hc, axis=-1, keepdims=True)
    hn = hc * lax.rsqrt(var + LN_EPS) * lng + lnb
    hn = hn * jax.nn.sigmoid(hn)
    return _dot(hn.astype(BF16), w)


CONV_HALO = 32
CONV_ROWS = 16


def _conv_seq_kernel(u_ref, halo_ref, x_ref, dw_ref, dwb_ref, lng_ref, lnb_ref, w_ref, o_ref,
                     up_ref, h_ref, *, tm):
    i = pl.program_id(1)
    up_ref[0, 0:CONV_HALO, :] = jnp.where(i > 0, halo_ref[0], 0.0)
    up_ref[0, CONV_HALO:, :] = u_ref[0]
    n_sh = tm + CONV_HALO - SUBLANES
    for s in range(1, SUBLANES):
        up_ref[s, 0:n_sh, :] = up_ref[0, s:s + n_sh, :]
    lead = CONV_HALO - (CONV_WIDTH - 1)
    for c in range(tm // CONV_ROWS):
        r0 = c * CONV_ROWS
        acc = jnp.zeros((CONV_ROWS, D_MODEL), F32)
        for j in range(CONV_WIDTH):
            s, a0 = (lead + j) % SUBLANES, r0 + (lead + j) // SUBLANES * SUBLANES
            acc = acc + up_ref[s, a0:a0 + CONV_ROWS, :] * dw_ref[j:j + 1, :]
        h_ref[r0:r0 + CONV_ROWS, :] = acc
    h = h_ref[...] + dwb_ref[...]
    o_ref[0] = x_ref[0] + _ln_silu_proj(h, lng_ref[...], lnb_ref[...], w_ref[...])


def conv_seq(u, x, dw, dwb, lng, lnb, w_out):
    b, s, d = u.shape
    tm = 128
    nt = s // tm
    hb = tm // CONV_HALO
    vec = lambda v: v.reshape(1, d)
    return pl.pallas_call(
        functools.partial(_conv_seq_kernel, tm=tm),
        out_shape=jax.ShapeDtypeStruct((b, s, d), F32),
        grid=(b, nt),
        in_specs=[
            pl.BlockSpec((1, tm, d), lambda bi, i: (bi, i, 0)),
            pl.BlockSpec((1, CONV_HALO, d), lambda bi, i: (bi, jnp.maximum(i * hb - 1, 0), 0)),
            pl.BlockSpec((1, tm, d), lambda bi, i: (bi, i, 0)),
            pl.BlockSpec((CONV_WIDTH, d), lambda bi, i: (0, 0)),
            pl.BlockSpec((1, d), lambda bi, i: (0, 0)),
            pl.BlockSpec((1, d), lambda bi, i: (0, 0)),
            pl.BlockSpec((1, d), lambda bi, i: (0, 0)),
            pl.BlockSpec((d, d), lambda bi, i: (0, 0)),
        ],
        out_specs=pl.BlockSpec((1, tm, d), lambda bi, i: (bi, i, 0)),
        scratch_shapes=[pltpu.VMEM((SUBLANES, tm + CONV_HALO, d), F32), pltpu.VMEM((tm, d), F32)],
        compiler_params=_params(("parallel", "arbitrary")),
        name="conv_seq",
    )(u, u, x, dw, vec(dwb), vec(lng), vec(lnb), w_out)


def _conv_step_kernel(u_ref, buf_ref, x_ref, dw_ref, dwb_ref, lng_ref, lnb_ref, w_ref, o_ref):
    nb = CONV_WIDTH - 1
    h = jnp.sum(buf_ref[...] * dw_ref[0:nb, :][None], axis=1)
    h = h + u_ref[...] * dw_ref[nb:nb + 1, :] + dwb_ref[...]
    o_ref[...] = x_ref[...] + _ln_silu_proj(h, lng_ref[...], lnb_ref[...], w_ref[...])


def conv_step(u, buf, x, dw, dwb, lng, lnb, w_out):
    b, d = u.shape
    tb = 32
    vec = lambda v: v.reshape(1, d)
    return pl.pallas_call(
        _conv_step_kernel,
        out_shape=jax.ShapeDtypeStruct((b, d), F32),
        grid=(b // tb,),
        in_specs=[
            _row_spec(tb, d),
            pl.BlockSpec((tb, CONV_WIDTH - 1, d), lambda i: (i, 0, 0)),
            _row_spec(tb, d),
            _full_spec((CONV_WIDTH, d)), _full_spec((1, d)), _full_spec((1, d)), _full_spec((1, d)),
            _full_spec((d, d)),
        ],
        out_specs=_row_spec(tb, d),
        compiler_params=_params(("parallel",)),
        name="conv_step",
    )(u, buf, x, dw, vec(dwb), vec(lng), vec(lnb), w_out)


POOL_HALO = 16


def _pool_project(d, w_ref, scale):
    ys = []
    for gi in range(len(POOL_WINDOWS)):
        dg = d[:, gi * POOL_GROUP:(gi + 1) * POOL_GROUP].astype(BF16)
        ys.append(_dot(dg, w_ref[gi]))
    return jnp.concatenate(ys, axis=-1) * scale


def _pool_seq_kernel(x_ref, halo_ref, g_ref, w_ref, sc_ref, o_ref, tail_ref, xc_ref, *, tm, nt):
    i = pl.program_id(1)
    g = g_ref[...]
    xn = _rms(x_ref[0], g)
    xc_ref[0:POOL_HALO, :] = jnp.where(i > 0, _rms(halo_ref[0], g), 0.0)
    xc_ref[POOL_HALO:, :] = xn
    pos = i * tm + lax.broadcasted_iota(jnp.int32, (tm, 1), 0)
    means = []
    for gi, w in enumerate(POOL_WINDOWS):
        c = slice(gi * POOL_GROUP, (gi + 1) * POOL_GROUP)
        win = xn[:, c]
        for k in range(1, w):
            win = win + xc_ref[POOL_HALO - k:POOL_HALO - k + tm, c]
        cnt = jnp.minimum(w, pos + 1).astype(F32)
        means.append(win / cnt)
    d = jnp.concatenate(means, axis=-1) - xn
    o_ref[0] = x_ref[0] + _pool_project(d, w_ref, sc_ref[...])

    @pl.when(i == nt - 1)
    def _():
        tail_ref[0] = xn[tm - POOL_HALO:, :]


def pool_seq(x, g, w_grp, scale):
    b, s, d = x.shape
    tm = 256
    nt = s // tm
    hb = tm // POOL_HALO
    return pl.pallas_call(
        functools.partial(_pool_seq_kernel, tm=tm, nt=nt),
        out_shape=(jax.ShapeDtypeStruct((b, s, d), F32), jax.ShapeDtypeStruct((b, POOL_HALO, d), F32)),
        grid=(b, nt),
        in_specs=[
            pl.BlockSpec((1, tm, d), lambda bi, i: (bi, i, 0)),
            pl.BlockSpec((1, POOL_HALO, d), lambda bi, i: (bi, jnp.maximum(i * hb - 1, 0), 0)),
            pl.BlockSpec((1, d), lambda bi, i: (0, 0)),
            pl.BlockSpec(w_grp.shape, lambda bi, i: (0, 0, 0)),
            pl.BlockSpec((1, d), lambda bi, i: (0, 0)),
        ],
        out_specs=(pl.BlockSpec((1, tm, d), lambda bi, i: (bi, i, 0)),
                   pl.BlockSpec((1, POOL_HALO, d), lambda bi, i: (bi, 0, 0))),
        scratch_shapes=[pltpu.VMEM((tm + POOL_HALO, d), F32)],
        compiler_params=_params(("parallel", "arbitrary")),
        name="pool_seq",
    )(x, x, g.reshape(1, d), w_grp, scale.reshape(1, d))


def _pool_step_kernel(x_ref, buf_ref, g_ref, msk_ref, icnt_ref, w_ref, sc_ref, o_ref, xn_ref):
    xn = _rms(x_ref[...], g_ref[...])
    win = xn + jnp.sum(buf_ref[...] * msk_ref[...][None], axis=1)
    d = win * icnt_ref[...] - xn
    o_ref[...] = x_ref[...] + _pool_project(d, w_ref, sc_ref[...])
    xn_ref[...] = xn


def pool_step(x, buf, start_pos, g, w_grp, scale):
    b, d = x.shape
    tb = 32
    rows = jnp.arange(POOL_BUF)[:, None]
    win_of_lane = jnp.repeat(jnp.asarray(POOL_WINDOWS), POOL_GROUP)[None, :]
    msk = (rows >= POOL_BUF - (win_of_lane - 1)).astype(F32)
    icnt = 1.0 / jnp.minimum(win_of_lane, start_pos + 1).astype(F32)
    return pl.pallas_call(
        _pool_step_kernel,
        out_shape=(jax.ShapeDtypeStruct((b, d), F32), jax.ShapeDtypeStruct((b, d), F32)),
        grid=(b // tb,),
        in_specs=[
            _row_spec(tb, d),
            pl.BlockSpec((tb, POOL_BUF, d), lambda i: (i, 0, 0)),
            _full_spec((1, d)), _full_spec((POOL_BUF, d)), _full_spec((1, d)),
            _full_spec(w_grp.shape), _full_spec((1, d)),
        ],
        out_specs=(_row_spec(tb, d), _row_spec(tb, d)),
        compiler_params=_params(("parallel",)),
        name="pool_step",
    )(x, buf, g.reshape(1, d), msk, icnt, w_grp, scale.reshape(1, d))


def _qkv_kernel(x_ref, g_ref, wq_ref, wk_ref, wv_ref, q_ref, k_ref, v_ref, kb_ref, vb_ref):
    xn = _rms(x_ref[...], g_ref[...]).astype(BF16)
    q_ref[...] = (_dot(xn, wq_ref[...]) * (SB_SCALE * LOG2E)).astype(BF16)
    k = _dot(xn, wk_ref[...])
    v = _dot(xn, wv_ref[...])
    k_ref[...] = k
    v_ref[...] = v
    kb_ref[...] = k.astype(BF16)
    vb_ref[...] = v.astype(BF16)


def qkv_proj(x, g, wq, wk, wv):
    m, d = x.shape
    tm = _row_tile(m, 512)
    sd = lambda dt: jax.ShapeDtypeStruct((m, d), dt)
    return pl.pallas_call(
        _qkv_kernel,
        out_shape=(sd(BF16), sd(F32), sd(F32), sd(BF16), sd(BF16)),
        grid=(m // tm,),
        in_specs=[_row_spec(tm, d), _full_spec((1, d)), _full_spec((d, d)), _full_spec((d, d)),
                  _full_spec((d, d))],
        out_specs=tuple(_row_spec(tm, d) for _ in range(5)),
        compiler_params=_params(("parallel",)),
        name="qkv_proj",
    )(x, g.reshape(1, d), wq, wk, wv)


def _qkv_seq_kernel(x_ref, g_ref, wq_ref, wkt_ref, wvt_ref, q_ref, kt_ref, vt_ref, ktb_ref, vtb_ref):
    xn = _rms(x_ref[0], g_ref[...]).astype(BF16)
    q_ref[0] = (_dot(xn, wq_ref[...]) * (SB_SCALE * LOG2E)).astype(BF16)
    kt = _dot_nt(wkt_ref[...], xn)
    vt = _dot_nt(wvt_ref[...], xn)
    kt_ref[0] = kt
    vt_ref[0] = vt
    ktb_ref[0] = kt.astype(BF16)
    vtb_ref[0] = vt.astype(BF16)


def qkv_proj_seq(x, g, wq, wk_t, wv_t):
    b, s, d = x.shape
    tm = 512
    rows = pl.BlockSpec((1, tm, d), lambda bi, i: (bi, i, 0))
    cols = pl.BlockSpec((1, d, tm), lambda bi, i: (bi, 0, i))
    full = lambda shape: pl.BlockSpec(shape, lambda bi, i: (0, 0))
    tsd = lambda dt: jax.ShapeDtypeStruct((b, d, s), dt)
    return pl.pallas_call(
        _qkv_seq_kernel,
        out_shape=(jax.ShapeDtypeStruct((b, s, d), BF16), tsd(F32), tsd(F32), tsd(BF16), tsd(BF16)),
        grid=(b, s // tm),
        in_specs=[rows, full((1, d)), full((d, d)), full((d, d)), full((d, d))],
        out_specs=(rows, cols, cols, cols, cols),
        compiler_params=_params(("parallel", "parallel")),
        name="qkv_proj_seq",
    )(x, g.reshape(1, d), wq, wk_t, wv_t)


def _softplus(z):
    return jnp.maximum(z, 0.0) + jnp.log(1.0 + jnp.exp(-jnp.abs(z)))


ATT_BLK = 256
ATT_Q = 512
ATT_ROWS = 128
ATT_GROUP = 4
LOG2E = 1.4426950408889634


def _softplus2(z):
    neg_abs = lax.bitcast_convert_type(
        lax.bitcast_convert_type(z, jnp.int32) | jnp.int32(-2 ** 31), F32)
    return jnp.maximum(z, 0.0) + jnp.log2(1.0 + jnp.exp2(neg_abs))


def _sb_prompt_kernel(bias_ref, q_ref, k_ref, v_ref, o_ref, z_scr, x_scr, p_scr):
    hp = pl.program_id(1)
    i = pl.program_id(2)
    t = ATT_BLK
    nd = ATT_Q // t
    nr = ATT_Q // ATT_ROWS
    row = lax.broadcasted_iota(jnp.int32, (t, t), 0)
    col = lax.broadcasted_iota(jnp.int32, (t, t), 1)
    neg_upper = jnp.where(row > col, -1.0, 0.0).astype(BF16)
    q_off = lax.broadcasted_iota(jnp.int32, (ATT_ROWS, t), 0)
    k_off = lax.broadcasted_iota(jnp.int32, (ATT_ROWS, t), 1)
    heads = [slice(hh * HEAD_DIM, (hh + 1) * HEAD_DIM) for hh in range(2)]
    rows = [slice(r * ATT_ROWS, (r + 1) * ATT_ROWS) for r in range(nr)]
    chains = [(hh, r) for hh in range(2) for r in range(nr)]
    nc = len(chains)
    bias = [bias_ref[2 * hp + hh] * LOG2E for hh in range(2)]
    q = [q_ref[0, rows[r], heads[hh]] for hh, r in chains]

    def logits(blk, skip=()):
        kb = k_ref[0, :, pl.ds(pl.multiple_of(blk * t, t), t)]
        return [None if r in skip else _dot(q[ci], kb[heads[hh], :]) + bias[hh]
                for ci, (hh, r) in enumerate(chains)]

    def stash_logits(zs):
        for ci in range(nc):
            z_scr[ci] = zs[ci]

    def stash_weights(zs, cs, masks, skip=()):
        cs_new = list(cs)
        for g0 in range(0, nc, ATT_GROUP):
            group = [ci for ci in range(g0, g0 + ATT_GROUP) if chains[ci][1] not in skip]
            sps, sp0 = [], []
            for ci in group:
                r = chains[ci][1]
                sp = _softplus2(zs[ci])
                x_scr[ci] = zs[ci] - sp
                sp = sp if masks[r] is None else jnp.where(masks[r], sp, 0.0)
                sps.append(sp.astype(BF16))
                sp0.append(sp[:, 0:1])
            betweens = [_dot(sp, neg_upper) for sp in sps]
            for ci, bt, s0 in zip(group, betweens, sp0):
                r = chains[ci][1]
                p = jnp.exp2(x_scr[ci] + bt + cs[ci])
                p_scr[ci] = (p if masks[r] is None else jnp.where(masks[r], p, 0.0)).astype(BF16)
                cs_new[ci] = cs[ci] + bt[:, 0:1] - s0
        return cs_new

    def weighted_values(blk, accs, skip=()):
        vb = v_ref[0, :, pl.ds(pl.multiple_of(blk * t, t), t)]
        return [acc if r in skip else acc + _dot_nt(p_scr[ci], vb[heads[hh], :])
                for ci, (acc, (hh, r)) in enumerate(zip(accs, chains))]

    cs = [jnp.zeros((ATT_ROWS, 1), F32)] * nc
    accs = [jnp.zeros((ATT_ROWS, HEAD_DIM), F32)] * nc
    unseen = lambda dblk: [r for r in range(nr) if (r + 1) * ATT_ROWS - 1 <= dblk * t]
    zs = logits(nd * i + nd - 1, unseen(nd - 1))
    for dblk in reversed(range(nd)):
        blk = nd * i + dblk
        z_next = logits(jnp.maximum(blk - 1, 0), unseen(dblk - 1) if dblk else ())
        if dblk < nd - 1:
            accs = weighted_values(blk + 1, accs, unseen(dblk + 1))
        masks = [None if r * ATT_ROWS > dblk * t + t - 1 else (dblk * t + k_off) < (r * ATT_ROWS + q_off)
                 for r in range(nr)]
        cs = stash_weights(zs, cs, masks, unseen(dblk))
        zs = z_next
    stash_logits(zs)

    def body(n, carry):
        cs, accs = carry
        blk = nd * i - 1 - n
        zs = [z_scr[ci] for ci in range(nc)]
        stash_logits(logits(jnp.maximum(blk - 1, 0)))
        accs = weighted_values(blk + 1, accs)
        cs = stash_weights(zs, cs, [None] * nr)
        return cs, accs

    cs, accs = lax.fori_loop(0, nd * i, body, (cs, accs))
    accs = weighted_values(0, accs)
    o_ref[0] = jnp.concatenate(
        [jnp.concatenate([accs[hh * nr + r] for r in range(nr)], axis=0) for hh in range(2)], axis=-1)


def sb_attn_prompt(q, k, v, bias):
    b, s, d = q.shape
    nc = 2 * ATT_Q // ATT_ROWS
    return pl.pallas_call(
        _sb_prompt_kernel,
        out_shape=jax.ShapeDtypeStruct((b, s, d), F32),
        grid=(b, N_HEADS // 2, s // ATT_Q),
        in_specs=[
            pl.BlockSpec(memory_space=pltpu.SMEM),
            pl.BlockSpec((1, ATT_Q, LANES), lambda bi, hp, i: (bi, i, hp)),
            pl.BlockSpec((1, LANES, s), lambda bi, hp, i: (bi, hp, 0)),
            pl.BlockSpec((1, LANES, s), lambda bi, hp, i: (bi, hp, 0)),
        ],
        out_specs=pl.BlockSpec((1, ATT_Q, LANES), lambda bi, hp, i: (bi, i, hp)),
        scratch_shapes=[pltpu.VMEM((nc, ATT_ROWS, ATT_BLK), F32), pltpu.VMEM((nc, ATT_ROWS, ATT_BLK), F32),
                        pltpu.VMEM((nc, ATT_ROWS, ATT_BLK), BF16)],
        compiler_params=_params(("parallel", "parallel", "arbitrary")),
        name="sb_attn_prompt",
    )(bias, q, k, v)


DEC_PAGES = 8


def _sb_decode_kernel(pt_ref, q_ref, kn_ref, vn_ref, bias_ref, *refs, n_steps):
    del pt_ref
    kp_refs, vp_refs = refs[:DEC_PAGES], refs[DEC_PAGES:2 * DEC_PAGES]
    o_ref, c_ref, qcol_ref, acc_ref, onew_ref = refs[2 * DEC_PAGES:]
    p = pl.program_id(1)
    t = PAGE_SIZE
    dh = HEAD_DIM
    bias = bias_ref[...] * LOG2E
    eye = (lax.broadcasted_iota(jnp.int32, (1, dh, dh), 1) == lax.broadcasted_iota(jnp.int32, (1, dh, dh), 2))

    @pl.when(p == 0)
    def _():
        q = q_ref[0]
        zn = jnp.sum(kn_ref[0] * q, axis=-1, keepdims=True) + bias
        q_pos = n_steps * DEC_PAGES * PAGE_SIZE
        visible = (q_pos + lax.broadcasted_iota(jnp.int32, zn.shape, 1)) < q_pos
        spn = _softplus2(zn)
        c_ref[...] = jnp.where(visible, -spn, 0.0)
        onew_ref[...] = jnp.where(visible, jnp.exp2(zn - spn), 0.0) * vn_ref[0]
        acc_ref[...] = jnp.zeros_like(acc_ref)
        q_col = jnp.sum(jnp.where(eye, q[:, None, :], 0.0), axis=-1, keepdims=True)
        qcol_ref[...] = jnp.broadcast_to(q_col, qcol_ref.shape)

    row = lax.broadcasted_iota(jnp.int32, (t, t), 0)
    col = lax.broadcasted_iota(jnp.int32, (t, t), 1)
    neg_upper = jnp.where(row > col, -1.0, 0.0).astype(BF16)
    c = c_ref[...]
    q_col = qcol_ref[...]
    for g in range(DEC_PAGES):
        z = jnp.sum(kp_refs[g][0] * q_col, axis=1) + bias
        sp = _softplus2(z)
        between = _dot(sp.astype(BF16), neg_upper)
        a = jnp.exp2(z - sp + between + c)
        acc_ref[...] += vp_refs[g][0] * a[:, None, :]
        c = c + between[:, 0:1] - sp[:, 0:1]
    c_ref[...] = c

    @pl.when(p == n_steps - 1)
    def _():
        o_col = jnp.sum(acc_ref[...], axis=-1, keepdims=True)
        o_ref[0] = jnp.sum(jnp.where(eye, o_col, 0.0), axis=1) + onew_ref[...]


def sb_attn_decode(q, k_new, v_new, bias, cache_k, cache_v, page_table):
    b = q.shape[0]
    n_pages = page_table.shape[1]
    n_steps = n_pages // DEC_PAGES
    hd = (N_HEADS, HEAD_DIM)

    def page(g):
        return lambda bi, p, pt: (pt[bi * n_pages + (n_pages - 1 - (p * DEC_PAGES + g))], 0, 0, 0)

    row_spec = pl.BlockSpec((1,) + hd, lambda bi, p, pt: (bi, 0, 0))
    page_specs = [pl.BlockSpec((1,) + hd + (PAGE_SIZE,), page(g)) for g in range(DEC_PAGES)]
    return pl.pallas_call(
        functools.partial(_sb_decode_kernel, n_steps=n_steps),
        out_shape=jax.ShapeDtypeStruct((b,) + hd, F32),
        grid_spec=pltpu.PrefetchScalarGridSpec(
            num_scalar_prefetch=1,
            grid=(b, n_steps),
            in_specs=[row_spec, row_spec, row_spec, pl.BlockSpec((N_HEADS, 1), lambda bi, p, pt: (0, 0))]
            + page_specs + page_specs,
            out_specs=row_spec,
            scratch_shapes=[pltpu.VMEM((N_HEADS, 1), F32), pltpu.VMEM(hd + (PAGE_SIZE,), F32),
                            pltpu.VMEM(hd + (PAGE_SIZE,), F32), pltpu.VMEM(hd, F32)],
        ),
        compiler_params=_params(("parallel", "arbitrary")),
        name="sb_attn_decode",
    )(page_table.reshape(-1), q, k_new, v_new, bias.reshape(N_HEADS, 1),
      *([cache_k] * DEC_PAGES), *([cache_v] * DEC_PAGES))


def _head_sum(x, sel, expand):
    return _dot_x2(_dot_x2(x, sel), expand)


def _rwkv_in_kernel(x_ref, prev_ref, g_ref, mix_ref, wr_ref, wk_ref, wv_ref, w0_ref, w1_ref, w2_ref,
                    a0_ref, a1_ref, a2_ref, g1_ref, g2_ref, kk_ref, ka_ref, sel_ref, exp_ref,
                    r_o, lw_o, k_o, v_o, kk_o, a_o, g_o, xn_o):
    xn = _rms(x_ref[...], g_ref[...])
    xx = prev_ref[...] - xn
    mixed = lambda c: (xn + xx * mix_ref[c:c + 1, :]).astype(BF16)
    xr, xw, xk, xv, xa, xg = (mixed(c) for c in range(6))
    r = _dot(xr, wr_ref[...])
    w = w0_ref[...] + _dot(jnp.tanh(_dot(xw, w1_ref[...])).astype(BF16), w2_ref[...])
    w = -_softplus(-w) - 0.5
    k = _dot(xk, wk_ref[...])
    v = _dot(xv, wv_ref[...])
    a = jax.nn.sigmoid(a0_ref[...] + _dot(_dot(xa, a1_ref[...]).astype(BF16), a2_ref[...]))
    g = _dot(jax.nn.sigmoid(_dot(xg, g1_ref[...])).astype(BF16), g2_ref[...])
    kk = k * kk_ref[...]
    nrm = jnp.sqrt(_head_sum(kk * kk, sel_ref[...], exp_ref[...]))
    kk = kk / jnp.maximum(nrm, 1e-12)
    r_o[...] = r
    lw_o[...] = -jnp.exp(w)
    k_o[...] = k * (1.0 + (a - 1.0) * ka_ref[...])
    v_o[...] = v
    kk_o[...] = kk
    a_o[...] = a
    g_o[...] = g
    xn_o[...] = xn


def _head_selectors():
    lane = jnp.arange(D_MODEL)[:, None] // HEAD_DIM
    sel = (lane == jnp.arange(LANES)[None, :]).astype(BF16)
    return sel, sel.T


def rwkv_in(x, x_prev_rows, g, p):
    m, d = x.shape
    tm = _row_tile(m, 256)
    sel, expand = _head_selectors()
    vec = lambda v: v.reshape(1, d)
    ws = [p["mix"], p["w_r"], p["w_k"], p["w_v"], vec(p["w0"]), p["w1"], p["w2"], vec(p["a0"]), p["a1"],
          p["a2"], p["g1"], p["g2"], vec(p["k_k"]), vec(p["k_a"]), sel, expand]
    sd = jax.ShapeDtypeStruct((m, d), F32)
    return pl.pallas_call(
        _rwkv_in_kernel,
        out_shape=(sd,) * 8,
        grid=(m // tm,),
        in_specs=[_row_spec(tm, d), _row_spec(tm, d), _full_spec((1, d))] + [_full_spec(w.shape) for w in ws],
        out_specs=tuple(_row_spec(tm, d) for _ in range(8)),
        compiler_params=_params(("parallel",)),
        name="rwkv_in",
    )(x, x_prev_rows, vec(g), *ws)


def _norm_shift_kernel(x_ref, halo_ref, g_ref, o_ref, *, tm):
    i = pl.program_id(1)
    g = g_ref[...]
    xn = _rms(x_ref[0], g)
    last = _rms(halo_ref[0], g)[7:8, :]
    first = jnp.where(i > 0, last, 0.0)
    rolled = pltpu.roll(xn, 1, axis=0)
    row = lax.broadcasted_iota(jnp.int32, (tm, 1), 0)
    o_ref[0] = jnp.where(row == 0, first, rolled)


def norm_shift(x, g):
    b, s, d = x.shape
    tm = 512
    hb = tm // 8
    return pl.pallas_call(
        functools.partial(_norm_shift_kernel, tm=tm),
        out_shape=jax.ShapeDtypeStruct((b, s, d), F32),
        grid=(b, s // tm),
        in_specs=[
            pl.BlockSpec((1, tm, d), lambda bi, i: (bi, i, 0)),
            pl.BlockSpec((1, 8, d), lambda bi, i: (bi, jnp.maximum(i * hb - 1, 0), 0)),
            pl.BlockSpec((1, d), lambda bi, i: (0, 0)),
        ],
        out_specs=pl.BlockSpec((1, tm, d), lambda bi, i: (bi, i, 0)),
        compiler_params=_params(("parallel", "arbitrary")),
        name="norm_shift",
    )(x, x, g.reshape(1, d))


WKV_CHUNK = 64
WKV_UNROLL = 2

def _wkv_seq_kernel(r_ref, lw_ref, k_ref, v_ref, kk_ref, a_ref, o_ref, s_ref, h_ref, *, tb, nt):
    i = pl.program_id(1)
    c = WKV_CHUNK
    dh = HEAD_DIM

    @pl.when(i == 0)
    def _():
        h_ref[...] = jnp.zeros_like(h_ref)

    row = lax.broadcasted_iota(jnp.int32, (c, c), 0)
    col = lax.broadcasted_iota(jnp.int32, (c, c), 1)
    tri_incl = _ones_where(col <= row)
    eye = row == col
    row2 = lax.broadcasted_iota(jnp.int32, (2 * c, 2 * c), 0)
    col2 = lax.broadcasted_iota(jnp.int32, (2 * c, 2 * c), 1)
    colm = jnp.where(col2 < c, col2, col2 - c)
    gmask = colm < jnp.where(row2 < c, row2, row2 - c + 1)

    def chunk(n, carry):
        heads = [slice(h * dh, (h + 1) * dh) for h in range(N_HEADS)]
        sls, pre = [], []
        for j in range(WKV_UNROLL):
            sl = pl.ds(pl.multiple_of((n * WKV_UNROLL + j) * c, c), c)
            lw = lw_ref[0, sl, :]
            cl = _dot_x3(tri_incl, lw)
            kk = kk_ref[0, sl, :]
            cl_end = cl[c - 1:c, :]
            inv = jnp.exp(-cl)
            rt_f = r_ref[0, sl, :] * jnp.exp(cl)
            b = kk * a_ref[0, sl, :]
            k = k_ref[0, sl, :]
            to_end = jnp.exp(cl_end - cl)
            sls.append(sl)
            pre.append(dict(
                g_end=jnp.exp(cl_end), rt_f=rt_f, rt=rt_f.astype(BF16),
                at=(-kk * jnp.exp(cl - lw)).astype(BF16),
                bt=(b * inv).astype(BF16), kt=(k * inv).astype(BF16),
                be=(b * to_end).astype(BF16),
                ke=(k * to_end).astype(BF16),
                vb=v_ref[0, sl, :].astype(BF16)))
        units = [(p, ls) for p in pre for ls in heads]
        gms = [jnp.where(gmask, _dot_nt(jnp.concatenate([p["at"][:, ls], p["rt"][:, ls]], axis=0),
                                        jnp.concatenate([p["bt"][:, ls], p["kt"][:, ls]], axis=0)), 0.0)
               for p, ls in units]
        mabs = [gm[0:c, 0:c] for gm in gms]
        mvs = [_dot(gm[0:c, c:].astype(BF16), p["vb"][:, ls]) for gm, (p, ls) in zip(gms, units)]
        pws = [m.astype(BF16) for m in mabs]
        tms = [jnp.where(eye, 1.0, m) for m in mabs]
        for _ in range(5):
            pws = [_dot(pw, pw).astype(BF16) for pw in pws]
            tms = [tm_ + _dot(tm_.astype(BF16), pw) for tm_, pw in zip(tms, pws)]
        aws = [_dot(tm_.astype(BF16), jnp.concatenate([p["at"][:, ls], mv.astype(BF16)], axis=1))
               for tm_, mv, (p, ls) in zip(tms, mvs, units)]
        zmats = [jnp.concatenate([aw.astype(BF16),
                                  jnp.concatenate([jnp.zeros((c, dh), BF16), p["vb"][:, ls]], axis=1)], axis=0)
                 for aw, (p, ls) in zip(aws, units)]
        x1s = [_dot(gm[c:, :].astype(BF16), zm) for gm, zm in zip(gms, zmats)]
        x2s = [_dot_tn(jnp.concatenate([p["be"][:, ls], p["ke"][:, ls]], axis=0), zm)
               for zm, (p, ls) in zip(zmats, units)]
        lbs = [jnp.concatenate([p["rt_f"][:, ls] + x1[:, 0:dh],
                                jnp.where(eye, jnp.broadcast_to(p["g_end"][:, ls], (c, dh)), 0.0) + x2[:, 0:dh]],
                               axis=0).astype(BF16)
               for x1, x2, (p, ls) in zip(x1s, x2s, units)]
        for j in range(WKV_UNROLL):
            u0 = j * N_HEADS
            ress = []
            for h in range(N_HEADS):
                hi, lo = _split2(h_ref[h])
                ress.append(_dot(lbs[u0 + h], hi) + _dot(lbs[u0 + h], lo))
            for h, res in enumerate(ress):
                h_ref[h] = res[c:, :] + x2s[u0 + h][:, dh:]
            o_ref[0, sls[j], :] = jnp.concatenate(
                [res[0:c, :] + x1s[u0 + h][:, dh:] for h, res in enumerate(ress)], axis=-1)
        return carry

    lax.fori_loop(0, tb // (c * WKV_UNROLL), chunk, 0)

    @pl.when(i == nt - 1)
    def _():
        s_ref[0] = h_ref[...]


def wkv_seq(r, lw, k, v, kk, a):
    b, s, d = r.shape
    tb = 256
    nt = s // tb
    blk = pl.BlockSpec((1, tb, d), lambda bi, i: (bi, i, 0))
    return pl.pallas_call(
        functools.partial(_wkv_seq_kernel, tb=tb, nt=nt),
        out_shape=(jax.ShapeDtypeStruct((b, s, d), F32),
                   jax.ShapeDtypeStruct((b, N_HEADS, HEAD_DIM, HEAD_DIM), F32)),
        grid=(b, nt),
        in_specs=[blk] * 6,
        out_specs=(blk, pl.BlockSpec((1, N_HEADS, HEAD_DIM, HEAD_DIM), lambda bi, i: (bi, 0, 0, 0))),
        scratch_shapes=[pltpu.VMEM((N_HEADS, HEAD_DIM, HEAD_DIM), F32)],
        compiler_params=_params(("parallel", "arbitrary")),
        name="wkv_seq",
    )(r, lw, k, v, kk, a)


def _wkv_step_kernel(r_ref, lw_ref, k_ref, v_ref, kk_ref, a_ref, s0_ref, o_ref, s_ref, *, tb):
    dh = HEAD_DIM
    eye = (lax.broadcasted_iota(jnp.int32, (1, dh, dh), 1) == lax.broadcasted_iota(jnp.int32, (1, dh, dh), 2))
    bs = range(tb)
    along_keys = lambda ref, bi: ref[bi][:, None, :]
    s0 = [s0_ref[bi] for bi in bs]
    kk = [along_keys(kk_ref, bi) for bi in bs]
    v_col = [jnp.sum(jnp.where(eye, along_keys(v_ref, bi), 0.0), axis=-1, keepdims=True) for bi in bs]
    sa = [jnp.sum(s0[bi] * -kk[bi], axis=-1, keepdims=True) for bi in bs]
    s1 = [s0[bi] * jnp.exp(along_keys(lw_ref, bi)) + sa[bi] * (kk[bi] * along_keys(a_ref, bi))
          + v_col[bi] * along_keys(k_ref, bi) for bi in bs]
    for bi in bs:
        s_ref[bi] = s1[bi]
    o_col = [jnp.sum(s1[bi] * along_keys(r_ref, bi), axis=-1, keepdims=True) for bi in bs]
    for bi in bs:
        o_ref[bi] = jnp.sum(jnp.where(eye, o_col[bi], 0.0), axis=1)


def wkv_step(r, lw, k, v, kk, a, s0):
    b = r.shape[0]
    tb = 8
    hd = (N_HEADS, HEAD_DIM)
    rows = pl.BlockSpec((tb,) + hd, lambda i: (i, 0, 0))
    st = pl.BlockSpec((tb,) + hd + (HEAD_DIM,), lambda i: (i, 0, 0, 0))
    return pl.pallas_call(
        functools.partial(_wkv_step_kernel, tb=tb),
        out_shape=(jax.ShapeDtypeStruct((b,) + hd, F32), jax.ShapeDtypeStruct(s0.shape, F32)),
        grid=(b // tb,),
        in_specs=[rows] * 6 + [st],
        out_specs=(rows, st),
        compiler_params=_params(("parallel",)),
        name="wkv_step",
    )(r, lw, k, v, kk, a, s0)


def _rwkv_out_kernel(o_ref, r_ref, k_ref, v_ref, g_ref, x_ref, rk_ref, lng_ref, lnb_ref, sel_ref, exp_ref,
                     w_ref, y_ref):
    sel, expand = sel_ref[...], exp_ref[...]
    o = o_ref[...]
    mu = _head_sum(o, sel, expand) * (1.0 / HEAD_DIM)
    oc = o - mu
    var = _head_sum(oc * oc, sel, expand) * (1.0 / HEAD_DIM)
    on = oc * lax.rsqrt(var + GN_EPS) * lng_ref[...] + lnb_ref[...]
    bonus = _head_sum(r_ref[...] * k_ref[...] * rk_ref[...], sel, expand)
    on = on + bonus * v_ref[...]
    y_ref[...] = x_ref[...] + _dot((on * g_ref[...]).astype(BF16), w_ref[...])


def rwkv_out(o, r, k, v, g, x, r_k, ln_g, ln_b, w_o):
    m, d = x.shape
    tm = _row_tile(m, 256)
    sel, expand = _head_selectors()
    vec = lambda t: t.reshape(1, d)
    return pl.pallas_call(
        _rwkv_out_kernel,
        out_shape=jax.ShapeDtypeStruct((m, d), F32),
        grid=(m // tm,),
        in_specs=[_row_spec(tm, d)] * 6 + [_full_spec((1, d))] * 3
        + [_full_spec(sel.shape), _full_spec(expand.shape), _full_spec((d, d))],
        out_specs=_row_spec(tm, d),
        compiler_params=_params(("parallel",)),
        name="rwkv_out",
    )(o, r, k, v, g, x, vec(r_k), vec(ln_g), vec(ln_b), sel, expand, w_o)


def _trunk(x, seq, start_pos, conv_buf, pool_buf, kv_past, wkv0, shift0, w):
    b, t, d = x.shape
    m = b * t
    flat = lambda z: z.reshape(m, d)
    x2 = flat(x)

    u = conv_in(x2, w["norm_mix"][0], w["conv_wa"], w["conv_wb"])
    cw = (w["conv_dw"], w["conv_dw_b"], w["conv_ln_g"], w["conv_ln_b"], w["conv_w_out"])
    if seq:
        x2 = flat(conv_seq(u.reshape(b, t, d), x, *cw))
        new_conv = u.reshape(b, t, d)[:, t - (CONV_WIDTH - 1):]
    else:
        x2 = conv_step(u, conv_buf, x2, *cw)
        new_conv = jnp.concatenate([conv_buf[:, 1:], u[:, None]], axis=1)
    x2 = mlp(x2, w["norm_mlp"][0], w["mlp_up"][0], w["mlp_down"][0])

    if seq:
        y, tail = pool_seq(x2.reshape(b, t, d), w["norm_mix"][1], w["pool_w"], w["pool_scale"])
        x2 = flat(y)
        new_pool = tail[:, POOL_HALO - POOL_BUF:]
    else:
        x2, xn = pool_step(x2, pool_buf, start_pos, w["norm_mix"][1], w["pool_w"], w["pool_scale"])
        new_pool = jnp.concatenate([pool_buf[:, 1:], xn[:, None]], axis=1)
    x2 = mlp(x2, w["norm_mlp"][1], w["mlp_up"][1], w["mlp_down"][1])

    if seq:
        q, kt, vt, ktb, vtb = qkv_proj_seq(x2.reshape(b, t, d), w["norm_mix"][2], w["attn_wq"],
                                           w["attn_wk_t"], w["attn_wv_t"])
        att = flat(sb_attn_prompt(q, ktb, vtb, w["attn_sb_bias"]))
        new_k = jnp.transpose(kt.reshape(b, N_HEADS, HEAD_DIM, t), (0, 3, 1, 2))
        new_v = jnp.transpose(vt.reshape(b, N_HEADS, HEAD_DIM, t), (0, 3, 1, 2))
    else:
        q, k, v, _, _ = qkv_proj(x2, w["norm_mix"][2], w["attn_wq"], w["attn_wk"], w["attn_wv"])
        cache_k, cache_v, page_table = kv_past
        hd = lambda z: z.astype(F32).reshape(b, N_HEADS, HEAD_DIM)
        att = sb_attn_decode(hd(q), hd(k), hd(v), w["attn_sb_bias"], cache_k, cache_v, page_table)
        att = att.reshape(b, d)
        new_k = k.reshape(b, t, N_HEADS, HEAD_DIM)
        new_v = v.reshape(b, t, N_HEADS, HEAD_DIM)
    x2 = proj_res(att, x2, w["attn_w_o"])
    x2 = mlp(x2, w["norm_mlp"][2], w["mlp_up"][2], w["mlp_down"][2])

    if seq:
        prev = flat(norm_shift(x2.reshape(b, t, d), w["norm_mix"][3]))
    else:
        prev = shift0
    r, lw, k2, v2, kk, a, g, xn = rwkv_in(x2, prev, w["norm_mix"][3], w["rw"])
    if seq:
        sh = lambda z: z.reshape(b, t, d)
        o, hstate = wkv_seq(sh(r), sh(lw), sh(k2), sh(v2), sh(kk), sh(a))
        o = flat(o)
        new_wkv = jnp.swapaxes(hstate, -1, -2)
        new_shift = xn.reshape(b, t, d)[:, t - 1]
    else:
        hd = lambda z: z.reshape(b, N_HEADS, HEAD_DIM)
        o, new_wkv = wkv_step(hd(r), hd(lw), hd(k2), hd(v2), hd(kk), hd(a), wkv0)
        o = o.reshape(b, d)
        new_shift = xn
    x2 = rwkv_out(o, r, k2, v2, g, x2, w["rw"]["r_k"], w["rw"]["ln_g"], w["rw"]["ln_b"], w["rw"]["w_o"])
    x2 = mlp(x2, w["norm_mlp"][3], w["mlp_up"][3], w["mlp_down"][3], g_final=w["norm_final"])

    return x2.reshape(b, t, d), new_conv, new_pool, new_k, new_v, new_wkv, new_shift


def kernel(x_prompt, x_sample, cache_conv, cache_pool, cache_k, cache_v, page_table, state_wkv, state_shift,
           norm_mix, norm_mlp, norm_final, mlp_w_up, mlp_w_down,
           conv_w_in, conv_dw, conv_dw_b, conv_ln_g, conv_ln_b, conv_w_out,
           pool_w, pool_scale, attn_w_qkv, attn_w_o, attn_sb_bias,
           rw_mix, rw_w_r, rw_w_k, rw_w_v, rw_w_o, rw_w0, rw_w1, rw_w2, rw_a0, rw_a1, rw_a2,
           rw_g1, rw_g2, rw_k_k, rw_k_a, rw_r_k, rw_ln_g, rw_ln_b):
    d = D_MODEL
    bf = lambda z: z.astype(BF16)
    gl = rw_g1.shape[-1]
    glp = -(-gl // LANES) * LANES
    g1 = jnp.pad(rw_g1[0], ((0, 0), (0, glp - gl)))
    g2 = jnp.pad(rw_g2[0], ((0, glp - gl), (0, 0)))
    w = dict(
        norm_mix=norm_mix, norm_mlp=norm_mlp, norm_final=norm_final,
        mlp_up=bf(mlp_w_up), mlp_down=bf(mlp_w_down),
        conv_wa=bf(conv_w_in[0, :, :d]), conv_wb=bf(conv_w_in[0, :, d:]),
        conv_dw=conv_dw[0], conv_dw_b=conv_dw_b[0], conv_ln_g=conv_ln_g[0], conv_ln_b=conv_ln_b[0],
        conv_w_out=bf(conv_w_out[0]),
        pool_w=bf(pool_w[0]), pool_scale=pool_scale[0],
        attn_wq=bf(attn_w_qkv[0, :, :d]), attn_wk=bf(attn_w_qkv[0, :, d:2 * d]),
        attn_wv=bf(attn_w_qkv[0, :, 2 * d:]), attn_w_o=bf(attn_w_o[0]), attn_sb_bias=attn_sb_bias[0],
        attn_wk_t=bf(attn_w_qkv[0, :, d:2 * d].T), attn_wv_t=bf(attn_w_qkv[0, :, 2 * d:].T),
        rw=dict(mix=rw_mix[0], w_r=bf(rw_w_r[0]), w_k=bf(rw_w_k[0]), w_v=bf(rw_w_v[0]), w_o=bf(rw_w_o[0]),
                w0=rw_w0[0], w1=bf(rw_w1[0]), w2=bf(rw_w2[0]), a0=rw_a0[0], a1=bf(rw_a1[0]), a2=bf(rw_a2[0]),
                g1=bf(g1), g2=bf(g2), k_k=rw_k_k[0], k_a=rw_k_a[0], r_k=rw_r_k[0].reshape(-1),
                ln_g=rw_ln_g[0], ln_b=rw_ln_b[0]),
    )
    yp, conv_p, pool_p, k_p, v_p, wkv_p, shift_p = _trunk(
        x_prompt, True, 0, None, None, None, None, None, w)
    past_len = page_table.shape[1] * PAGE_SIZE
    ys, conv_s, pool_s, k_s, v_s, wkv_s, shift_s = _trunk(
        x_sample, False, past_len, cache_conv[0], cache_pool[0],
        (jnp.transpose(cache_k[0], (0, 2, 3, 1)), jnp.transpose(cache_v[0], (0, 2, 3, 1)), page_table),
        state_wkv[0], state_shift[0], w)
    st = lambda z: z[None]
    return (yp, ys, st(conv_p), st(conv_s), st(pool_p), st(pool_s), st(k_p), st(v_p), st(k_s), st(v_s),
            st(wkv_p), st(wkv_s), st(shift_p), st(shift_s))
```

```python
import functools

import jax
import jax.numpy as jnp
from jax import lax
from jax.experimental import pallas as pl
from jax.experimental.pallas import tpu as pltpu

F32 = jnp.float32
BF16 = jnp.bfloat16

D_MODEL = 1024
D_FF = 4 * D_MODEL
HEAD_DIM = 64
N_HEADS = D_MODEL // HEAD_DIM
CONV_WIDTH = 31
POOL_WINDOWS = (2, 4, 8, 16)
POOL_GROUP = D_MODEL // len(POOL_WINDOWS)
POOL_BUF = max(POOL_WINDOWS) - 1
PAGE_SIZE = 128
RMS_EPS = 1e-6
LN_EPS = 1e-5
GN_EPS = 1e-5 * HEAD_DIM
SB_SCALE = HEAD_DIM ** -0.5

LANES = 128
SUBLANES = 8
MXU_DIM = 256
VMEM_LIMIT = 48 << 20


def _params(sem, vmem=VMEM_LIMIT):
    return pltpu.CompilerParams(dimension_semantics=sem, vmem_limit_bytes=vmem)


def _rms(x, g):
    return x * lax.rsqrt(jnp.mean(x * x, axis=-1, keepdims=True) + RMS_EPS) * g


def _dot(a, b):
    return jnp.dot(a, b, preferred_element_type=F32)


def _dot_nt(a, b):
    return lax.dot_general(a, b, (((1,), (1,)), ((), ())), preferred_element_type=F32)


def _dot_tn(a, b):
    return lax.dot_general(a, b, (((0,), (0,)), ((), ())), preferred_element_type=F32)


def _ones_where(cond):
    return jnp.where(cond, 1.0, 0.0).astype(BF16)


def _split2(x):
    hi = x.astype(BF16)
    lo = (x - hi.astype(F32)).astype(BF16)
    return hi, lo


def _dot_x2(x, w):
    hi, lo = _split2(x)
    return _dot(hi, w) + _dot(lo, w)


def _dot_x3(w, x):
    hi = x.astype(BF16)
    r1 = x - hi.astype(F32)
    mid = r1.astype(BF16)
    lo = (r1 - mid.astype(F32)).astype(BF16)
    return _dot(w, hi) + _dot(w, mid) + _dot(w, lo)


def _row_spec(tm, n):
    return pl.BlockSpec((tm, n), lambda i: (i, 0))


def _full_spec(shape):
    return pl.BlockSpec(shape, lambda *_: (0,) * len(shape))


def _row_tile(m, pref):
    return pref if m % pref == 0 else m


def _mlp_kernel(x_ref, g_ref, gf_ref, wu_ref, wd_ref, o_ref, xn_ref, *, nj, final_norm):
    j = pl.program_id(1)

    @pl.when(j == 0)
    def _():
        x = x_ref[...]
        xn_ref[...] = _rms(x, g_ref[...]).astype(BF16)
        o_ref[...] = x

    h = _dot(xn_ref[...], wu_ref[...])
    h = jnp.square(jnp.maximum(h, 0.0)).astype(BF16)
    o_ref[...] += _dot(h, wd_ref[...])

    if final_norm:
        @pl.when(j == nj - 1)
        def _():
            o_ref[...] = _rms(o_ref[...], gf_ref[...])


def mlp(x, g, w_up, w_down, g_final=None):
    m, d = x.shape
    tm = _row_tile(m, 1024)
    tf = 1024
    nj = D_FF // tf
    final_norm = g_final is not None
    gf = g_final if final_norm else g
    return pl.pallas_call(
        functools.partial(_mlp_kernel, nj=nj, final_norm=final_norm),
        out_shape=jax.ShapeDtypeStruct((m, d), F32),
        grid=(m // tm, nj),
        in_specs=[
            pl.BlockSpec((tm, d), lambda i, j: (i, 0)),
            pl.BlockSpec((1, d), lambda i, j: (0, 0)),
            pl.BlockSpec((1, d), lambda i, j: (0, 0)),
            pl.BlockSpec((d, tf), lambda i, j: (0, j)),
            pl.BlockSpec((tf, d), lambda i, j: (j, 0)),
        ],
        out_specs=pl.BlockSpec((tm, d), lambda i, j: (i, 0)),
        scratch_shapes=[pltpu.VMEM((tm, d), BF16)],
        compiler_params=_params(("parallel", "arbitrary")),
        name="mlp",
    )(x, g.reshape(1, d), gf.reshape(1, d), w_up, w_down)


def _proj_res_kernel(a_ref, x_ref, w_ref, o_ref):
    o_ref[...] = x_ref[...] + _dot(a_ref[...].astype(BF16), w_ref[...])


def proj_res(a, x, w):
    m, d = x.shape
    tm = _row_tile(m, 512)
    return pl.pallas_call(
        _proj_res_kernel,
        out_shape=jax.ShapeDtypeStruct((m, d), F32),
        grid=(m // tm,),
        in_specs=[_row_spec(tm, a.shape[1]), _row_spec(tm, d), _full_spec(w.shape)],
        out_specs=_row_spec(tm, d),
        compiler_params=_params(("parallel",)),
        name="proj_res",
    )(a, x, w)


def _conv_in_kernel(x_ref, g_ref, wa_ref, wb_ref, o_ref):
    xn = _rms(x_ref[...], g_ref[...]).astype(BF16)
    o_ref[...] = _dot(xn, wa_ref[...]) * jax.nn.sigmoid(_dot(xn, wb_ref[...]))


def conv_in(x, g, w_a, w_b):
    m, d = x.shape
    tm = _row_tile(m, 512)
    return pl.pallas_call(
        _conv_in_kernel,
        out_shape=jax.ShapeDtypeStruct((m, d), F32),
        grid=(m // tm,),
        in_specs=[_row_spec(tm, d), _full_spec((1, d)), _full_spec(w_a.shape), _full_spec(w_b.shape)],
        out_specs=_row_spec(tm, d),
        compiler_params=_params(("parallel",)),
        name="conv_in",
    )(x, g.reshape(1, d), w_a, w_b)


def _ln_silu_proj(h, lng, lnb, w):
    mu = jnp.mean(h, axis=-1, keepdims=True)
    hc = h - mu
    var = jnp.mean(hc * hc, axis=-1, keepdims=True)
    hn = hc * lax.rsqrt(var + LN_EPS) * lng + lnb
    hn = hn * jax.nn.sigmoid(hn)
    return _dot(hn.astype(BF16), w)


CONV_HALO = 32
CONV_ROWS = 16


def _conv_seq_kernel(u_ref, halo_ref, x_ref, dw_ref, dwb_ref, lng_ref, lnb_ref, w_ref, o_ref,
                     up_ref, h_ref, *, tm):
    i = pl.program_id(1)
    up_ref[0, 0:CONV_HALO, :] = jnp.where(i > 0, halo_ref[0], 0.0)
    up_ref[0, CONV_HALO:, :] = u_ref[0]
    n_sh = tm + CONV_HALO - SUBLANES
    for s in range(1, SUBLANES):
        up_ref[s, 0:n_sh, :] = up_ref[0, s:s + n_sh, :]
    lead = CONV_HALO - (CONV_WIDTH - 1)
    for c in range(tm // CONV_ROWS):
        r0 = c * CONV_ROWS
        acc = jnp.zeros((CONV_ROWS, D_MODEL), F32)
        for j in range(CONV_WIDTH):
            s, a0 = (lead + j) % SUBLANES, r0 + (lead + j) // SUBLANES * SUBLANES
            acc = acc + up_ref[s, a0:a0 + CONV_ROWS, :] * dw_ref[j:j + 1, :]
        h_ref[r0:r0 + CONV_ROWS, :] = acc
    h = h_ref[...] + dwb_ref[...]
    o_ref[0] = x_ref[0] + _ln_silu_proj(h, lng_ref[...], lnb_ref[...], w_ref[...])


def conv_seq(u, x, dw, dwb, lng, lnb, w_out):
    b, s, d = u.shape
    tm = 128
    nt = s // tm
    hb = tm // CONV_HALO
    vec = lambda v: v.reshape(1, d)
    return pl.pallas_call(
        functools.partial(_conv_seq_kernel, tm=tm),
        out_shape=jax.ShapeDtypeStruct((b, s, d), F32),
        grid=(b, nt),
        in_specs=[
            pl.BlockSpec((1, tm, d), lambda bi, i: (bi, i, 0)),
            pl.BlockSpec((1, CONV_HALO, d), lambda bi, i: (bi, jnp.maximum(i * hb - 1, 0), 0)),
            pl.BlockSpec((1, tm, d), lambda bi, i: (bi, i, 0)),
            pl.BlockSpec((CONV_WIDTH, d), lambda bi, i: (0, 0)),
            pl.BlockSpec((1, d), lambda bi, i: (0, 0)),
            pl.BlockSpec((1, d), lambda bi, i: (0, 0)),
            pl.BlockSpec((1, d), lambda bi, i: (0, 0)),
            pl.BlockSpec((d, d), lambda bi, i: (0, 0)),
        ],
        out_specs=pl.BlockSpec((1, tm, d), lambda bi, i: (bi, i, 0)),
        scratch_shapes=[pltpu.VMEM((SUBLANES, tm + CONV_HALO, d), F32), pltpu.VMEM((tm, d), F32)],
        compiler_params=_params(("parallel", "arbitrary")),
        name="conv_seq",
    )(u, u, x, dw, vec(dwb), vec(lng), vec(lnb), w_out)


def _conv_step_kernel(u_ref, buf_ref, x_ref, dw_ref, dwb_ref, lng_ref, lnb_ref, w_ref, o_ref):
    nb = CONV_WIDTH - 1
    h = jnp.sum(buf_ref[...] * dw_ref[0:nb, :][None], axis=1)
    h = h + u_ref[...] * dw_ref[nb:nb + 1, :] + dwb_ref[...]
    o_ref[...] = x_ref[...] + _ln_silu_proj(h, lng_ref[...], lnb_ref[...], w_ref[...])


def conv_step(u, buf, x, dw, dwb, lng, lnb, w_out):
    b, d = u.shape
    tb = 32
    vec = lambda v: v.reshape(1, d)
    return pl.pallas_call(
        _conv_step_kernel,
        out_shape=jax.ShapeDtypeStruct((b, d), F32),
        grid=(b // tb,),
        in_specs=[
            _row_spec(tb, d),
            pl.BlockSpec((tb, CONV_WIDTH - 1, d), lambda i: (i, 0, 0)),
            _row_spec(tb, d),
            _full_spec((CONV_WIDTH, d)), _full_spec((1, d)), _full_spec((1, d)), _full_spec((1, d)),
            _full_spec((d, d)),
        ],
        out_specs=_row_spec(tb, d),
        compiler_params=_params(("parallel",)),
        name="conv_step",
    )(u, buf, x, dw, vec(dwb), vec(lng), vec(lnb), w_out)


POOL_HALO = 16


def _pool_project(d, w_ref, scale):
    ys = []
    for gi in range(len(POOL_WINDOWS)):
        dg = d[:, gi * POOL_GROUP:(gi + 1) * POOL_GROUP].astype(BF16)
        ys.append(_dot(dg, w_ref[gi]))
    return jnp.concatenate(ys, axis=-1) * scale


def _pool_seq_kernel(x_ref, halo_ref, g_ref, w_ref, sc_ref, o_ref, tail_ref, xc_ref, *, tm, nt):
    i = pl.program_id(1)
    g = g_ref[...]
    xn = _rms(x_ref[0], g)
    xc_ref[0:POOL_HALO, :] = jnp.where(i > 0, _rms(halo_ref[0], g), 0.0)
    xc_ref[POOL_HALO:, :] = xn
    pos = i * tm + lax.broadcasted_iota(jnp.int32, (tm, 1), 0)
    means = []
    for gi, w in enumerate(POOL_WINDOWS):
        c = slice(gi * POOL_GROUP, (gi + 1) * POOL_GROUP)
        win = xn[:, c]
        for k in range(1, w):
            win = win + xc_ref[POOL_HALO - k:POOL_HALO - k + tm, c]
        cnt = jnp.minimum(w, pos + 1).astype(F32)
        means.append(win / cnt)
    d = jnp.concatenate(means, axis=-1) - xn
    o_ref[0] = x_ref[0] + _pool_project(d, w_ref, sc_ref[...])

    @pl.when(i == nt - 1)
    def _():
        tail_ref[0] = xn[tm - POOL_HALO:, :]


def pool_seq(x, g, w_grp, scale):
    b, s, d = x.shape
    tm = 256
    nt = s // tm
    hb = tm // POOL_HALO
    return pl.pallas_call(
        functools.partial(_pool_seq_kernel, tm=tm, nt=nt),
        out_shape=(jax.ShapeDtypeStruct((b, s, d), F32), jax.ShapeDtypeStruct((b, POOL_HALO, d), F32)),
        grid=(b, nt),
        in_specs=[
            pl.BlockSpec((1, tm, d), lambda bi, i: (bi, i, 0)),
            pl.BlockSpec((1, POOL_HALO, d), lambda bi, i: (bi, jnp.maximum(i * hb - 1, 0), 0)),
            pl.BlockSpec((1, d), lambda bi, i: (0, 0)),
            pl.BlockSpec(w_grp.shape, lambda bi, i: (0, 0, 0)),
            pl.BlockSpec((1, d), lambda bi, i: (0, 0)),
        ],
        out_specs=(pl.BlockSpec((1, tm, d), lambda bi, i: (bi, i, 0)),
                   pl.BlockSpec((1, POOL_HALO, d), lambda bi, i: (bi, 0, 0))),
        scratch_shapes=[pltpu.VMEM((tm + POOL_HALO, d), F32)],
        compiler_params=_params(("parallel", "arbitrary")),
        name="pool_seq",
    )(x, x, g.reshape(1, d), w_grp, scale.reshape(1, d))


def _pool_step_kernel(x_ref, buf_ref, g_ref, msk_ref, icnt_ref, w_ref, sc_ref, o_ref, xn_ref):
    xn = _rms(x_ref[...], g_ref[...])
    win = xn + jnp.sum(buf_ref[...] * msk_ref[...][None], axis=1)
    d = win * icnt_ref[...] - xn
    o_ref[...] = x_ref[...] + _pool_project(d, w_ref, sc_ref[...])
    xn_ref[...] = xn


def pool_step(x, buf, start_pos, g, w_grp, scale):
    b, d = x.shape
    tb = 32
    rows = jnp.arange(POOL_BUF)[:, None]
    win_of_lane = jnp.repeat(jnp.asarray(POOL_WINDOWS), POOL_GROUP)[None, :]
    msk = (rows >= POOL_BUF - (win_of_lane - 1)).astype(F32)
    icnt = 1.0 / jnp.minimum(win_of_lane, start_pos + 1).astype(F32)
    return pl.pallas_call(
        _pool_step_kernel,
        out_shape=(jax.ShapeDtypeStruct((b, d), F32), jax.ShapeDtypeStruct((b, d), F32)),
        grid=(b // tb,),
        in_specs=[
            _row_spec(tb, d),
            pl.BlockSpec((tb, POOL_BUF, d), lambda i: (i, 0, 0)),
            _full_spec((1, d)), _full_spec((POOL_BUF, d)), _full_spec((1, d)),
            _full_spec(w_grp.shape), _full_spec((1, d)),
        ],
        out_specs=(_row_spec(tb, d), _row_spec(tb, d)),
        compiler_params=_params(("parallel",)),
        name="pool_step",
    )(x, buf, g.reshape(1, d), msk, icnt, w_grp, scale.reshape(1, d))


def _qkv_kernel(x_ref, g_ref, wq_ref, wk_ref, wv_ref, q_ref, k_ref, v_ref, kb_ref, vb_ref):
    xn = _rms(x_ref[...], g_ref[...]).astype(BF16)
    q_ref[...] = (_dot(xn, wq_ref[...]) * (SB_SCALE * LOG2E)).astype(BF16)
    k = _dot(xn, wk_ref[...])
    v = _dot(xn, wv_ref[...])
    k_ref[...] = k
    v_ref[...] = v
    kb_ref[...] = k.astype(BF16)
    vb_ref[...] = v.astype(BF16)


def qkv_proj(x, g, wq, wk, wv):
    m, d = x.shape
    tm = _row_tile(m, 512)
    sd = lambda dt: jax.ShapeDtypeStruct((m, d), dt)
    return pl.pallas_call(
        _qkv_kernel,
        out_shape=(sd(BF16), sd(F32), sd(F32), sd(BF16), sd(BF16)),
        grid=(m // tm,),
        in_specs=[_row_spec(tm, d), _full_spec((1, d)), _full_spec((d, d)), _full_spec((d, d)),
                  _full_spec((d, d))],
        out_specs=tuple(_row_spec(tm, d) for _ in range(5)),
        compiler_params=_params(("parallel",)),
        name="qkv_proj",
    )(x, g.reshape(1, d), wq, wk, wv)


def _qkv_seq_kernel(x_ref, g_ref, wq_ref, wkt_ref, wvt_ref, q_ref, kt_ref, vt_ref, ktb_ref, vtb_ref):
    xn = _rms(x_ref[0], g_ref[...]).astype(BF16)
    q_ref[0] = (_dot(xn, wq_ref[...]) * (SB_SCALE * LOG2E)).astype(BF16)
    kt = _dot_nt(wkt_ref[...], xn)
    vt = _dot_nt(wvt_ref[...], xn)
    kt_ref[0] = kt
    vt_ref[0] = vt
    ktb_ref[0] = kt.astype(BF16)
    vtb_ref[0] = vt.astype(BF16)


def qkv_proj_seq(x, g, wq, wk_t, wv_t):
    b, s, d = x.shape
    tm = 512
    rows = pl.BlockSpec((1, tm, d), lambda bi, i: (bi, i, 0))
    cols = pl.BlockSpec((1, d, tm), lambda bi, i: (bi, 0, i))
    full = lambda shape: pl.BlockSpec(shape, lambda bi, i: (0, 0))
    tsd = lambda dt: jax.ShapeDtypeStruct((b, d, s), dt)
    return pl.pallas_call(
        _qkv_seq_kernel,
        out_shape=(jax.ShapeDtypeStruct((b, s, d), BF16), tsd(F32), tsd(F32), tsd(BF16), tsd(BF16)),
        grid=(b, s // tm),
        in_specs=[rows, full((1, d)), full((d, d)), full((d, d)), full((d, d))],
        out_specs=(rows, cols, cols, cols, cols),
        compiler_params=_params(("parallel", "parallel")),
        name="qkv_proj_seq",
    )(x, g.reshape(1, d), wq, wk_t, wv_t)


def _softplus(z):
    return jnp.maximum(z, 0.0) + jnp.log(1.0 + jnp.exp(-jnp.abs(z)))


ATT_BLK = 256
ATT_Q = 512
ATT_ROWS = 128
ATT_GROUP = 4
LOG2E = 1.4426950408889634


def _softplus2(z):
    neg_abs = lax.bitcast_convert_type(
        lax.bitcast_convert_type(z, jnp.int32) | jnp.int32(-2 ** 31), F32)
    return jnp.maximum(z, 0.0) + jnp.log2(1.0 + jnp.exp2(neg_abs))


def _sb_prompt_kernel(bias_ref, q_ref, k_ref, v_ref, o_ref, z_scr, x_scr, p_scr):
    hp = pl.program_id(1)
    i = pl.program_id(2)
    t = ATT_BLK
    nd = ATT_Q // t
    nr = ATT_Q // ATT_ROWS
    row = lax.broadcasted_iota(jnp.int32, (t, t), 0)
    col = lax.broadcasted_iota(jnp.int32, (t, t), 1)
    neg_upper = jnp.where(row > col, -1.0, 0.0).astype(BF16)
    q_off = lax.broadcasted_iota(jnp.int32, (ATT_ROWS, t), 0)
    k_off = lax.broadcasted_iota(jnp.int32, (ATT_ROWS, t), 1)
    heads = [slice(hh * HEAD_DIM, (hh + 1) * HEAD_DIM) for hh in range(2)]
    rows = [slice(r * ATT_ROWS, (r + 1) * ATT_ROWS) for r in range(nr)]
    chains = [(hh, r) for hh in range(2) for r in range(nr)]
    nc = len(chains)
    bias = [bias_ref[2 * hp + hh] * LOG2E for hh in range(2)]
    q = [q_ref[0, rows[r], heads[hh]] for hh, r in chains]

    def logits(blk, skip=()):
        kb = k_ref[0, :, pl.ds(pl.multiple_of(blk * t, t), t)]
        return [None if r in skip else _dot(q[ci], kb[heads[hh], :]) + bias[hh]
                for ci, (hh, r) in enumerate(chains)]

    def stash_logits(zs):
        for ci in range(nc):
            z_scr[ci] = zs[ci]

    def stash_weights(zs, cs, masks, skip=()):
        cs_new = list(cs)
        for g0 in range(0, nc, ATT_GROUP):
            group = [ci for ci in range(g0, g0 + ATT_GROUP) if chains[ci][1] not in skip]
            sps, sp0 = [], []
            for ci in group:
                r = chains[ci][1]
                sp = _softplus2(zs[ci])
                x_scr[ci] = zs[ci] - sp
                sp = sp if masks[r] is None else jnp.where(masks[r], sp, 0.0)
                sps.append(sp.astype(BF16))
                sp0.append(sp[:, 0:1])
            betweens = [_dot(sp, neg_upper) for sp in sps]
            for ci, bt, s0 in zip(group, betweens, sp0):
                r = chains[ci][1]
                p = jnp.exp2(x_scr[ci] + bt + cs[ci])
                p_scr[ci] = (p if masks[r] is None else jnp.where(masks[r], p, 0.0)).astype(BF16)
                cs_new[ci] = cs[ci] + bt[:, 0:1] - s0
        return cs_new

    def weighted_values(blk, accs, skip=()):
        vb = v_ref[0, :, pl.ds(pl.multiple_of(blk * t, t), t)]
        return [acc if r in skip else acc + _dot_nt(p_scr[ci], vb[heads[hh], :])
                for ci, (acc, (hh, r)) in enumerate(zip(accs, chains))]

    cs = [jnp.zeros((ATT_ROWS, 1), F32)] * nc
    accs = [jnp.zeros((ATT_ROWS, HEAD_DIM), F32)] * nc
    unseen = lambda dblk: [r for r in range(nr) if (r + 1) * ATT_ROWS - 1 <= dblk * t]
    zs = logits(nd * i + nd - 1, unseen(nd - 1))
    for dblk in reversed(range(nd)):
        blk = nd * i + dblk
        z_next = logits(jnp.maximum(blk - 1, 0), unseen(dblk - 1) if dblk else ())
        if dblk < nd - 1:
            accs = weighted_values(blk + 1, accs, unseen(dblk + 1))
        masks = [None if r * ATT_ROWS > dblk * t + t - 1 else (dblk * t + k_off) < (r * ATT_ROWS + q_off)
                 for r in range(nr)]
        cs = stash_weights(zs, cs, masks, unseen(dblk))
        zs = z_next
    stash_logits(zs)

    def body(n, carry):
        cs, accs = carry
        blk = nd * i - 1 - n
        zs = [z_scr[ci] for ci in range(nc)]
        stash_logits(logits(jnp.maximum(blk - 1, 0)))
        accs = weighted_values(blk + 1, accs)
        cs = stash_weights(zs, cs, [None] * nr)
        return cs, accs

    cs, accs = lax.fori_loop(0, nd * i, body, (cs, accs))
    accs = weighted_values(0, accs)
    o_ref[0] = jnp.concatenate(
        [jnp.concatenate([accs[hh * nr + r] for r in range(nr)], axis=0) for hh in range(2)], axis=-1)


def sb_attn_prompt(q, k, v, bias):
    b, s, d = q.shape
    nc = 2 * ATT_Q // ATT_ROWS
    return pl.pallas_call(
        _sb_prompt_kernel,
        out_shape=jax.ShapeDtypeStruct((b, s, d), F32),
        grid=(b, N_HEADS // 2, s // ATT_Q),
        in_specs=[
            pl.BlockSpec(memory_space=pltpu.SMEM),
            pl.BlockSpec((1, ATT_Q, LANES), lambda bi, hp, i: (bi, i, hp)),
            pl.BlockSpec((1, LANES, s), lambda bi, hp, i: (bi, hp, 0)),
            pl.BlockSpec((1, LANES, s), lambda bi, hp, i: (bi, hp, 0)),
        ],
        out_specs=pl.BlockSpec((1, ATT_Q, LANES), lambda bi, hp, i: (bi, i, hp)),
        scratch_shapes=[pltpu.VMEM((nc, ATT_ROWS, ATT_BLK), F32), pltpu.VMEM((nc, ATT_ROWS, ATT_BLK), F32),
                        pltpu.VMEM((nc, ATT_ROWS, ATT_BLK), BF16)],
        compiler_params=_params(("parallel", "parallel", "arbitrary")),
        name="sb_attn_prompt",
    )(bias, q, k, v)


DEC_PAGES = 8


def _sb_decode_kernel(pt_ref, q_ref, kn_ref, vn_ref, bias_ref, *refs, n_steps):
    del pt_ref
    kp_refs, vp_refs = refs[:DEC_PAGES], refs[DEC_PAGES:2 * DEC_PAGES]
    o_ref, c_ref, qcol_ref, acc_ref, onew_ref = refs[2 * DEC_PAGES:]
    p = pl.program_id(1)
    t = PAGE_SIZE
    dh = HEAD_DIM
    bias = bias_ref[...] * LOG2E
    eye = (lax.broadcasted_iota(jnp.int32, (1, dh, dh), 1) == lax.broadcasted_iota(jnp.int32, (1, dh, dh), 2))

    @pl.when(p == 0)
    def _():
        q = q_ref[0]
        zn = jnp.sum(kn_ref[0] * q, axis=-1, keepdims=True) + bias
        q_pos = n_steps * DEC_PAGES * PAGE_SIZE
        visible = (q_pos + lax.broadcasted_iota(jnp.int32, zn.shape, 1)) < q_pos
        spn = _softplus2(zn)
        c_ref[...] = jnp.where(visible, -spn, 0.0)
        onew_ref[...] = jnp.where(visible, jnp.exp2(zn - spn), 0.0) * vn_ref[0]
        acc_ref[...] = jnp.zeros_like(acc_ref)
        q_col = jnp.sum(jnp.where(eye, q[:, None, :], 0.0), axis=-1, keepdims=True)
        qcol_ref[...] = jnp.broadcast_to(q_col, qcol_ref.shape)

    row = lax.broadcasted_iota(jnp.int32, (t, t), 0)
    col = lax.broadcasted_iota(jnp.int32, (t, t), 1)
    neg_upper = jnp.where(row > col, -1.0, 0.0).astype(BF16)
    c = c_ref[...]
    q_col = qcol_ref[...]
    for g in range(DEC_PAGES):
        z = jnp.sum(kp_refs[g][0] * q_col, axis=1) + bias
        sp = _softplus2(z)
        between = _dot(sp.astype(BF16), neg_upper)
        a = jnp.exp2(z - sp + between + c)
        acc_ref[...] += vp_refs[g][0] * a[:, None, :]
        c = c + between[:, 0:1] - sp[:, 0:1]
    c_ref[...] = c

    @pl.when(p == n_steps - 1)
    def _():
        o_col = jnp.sum(acc_ref[...], axis=-1, keepdims=True)
        o_ref[0] = jnp.sum(jnp.where(eye, o_col, 0.0), axis=1) + onew_ref[...]


def sb_attn_decode(q, k_new, v_new, bias, cache_k, cache_v, page_table):
    b = q.shape[0]
    n_pages = page_table.shape[1]
    n_steps = n_pages // DEC_PAGES
    hd = (N_HEADS, HEAD_DIM)

    def page(g):
        return lambda bi, p, pt: (pt[bi * n_pages + (n_pages - 1 - (p * DEC_PAGES + g))], 0, 0, 0)

    row_spec = pl.BlockSpec((1,) + hd, lambda bi, p, pt: (bi, 0, 0))
    page_specs = [pl.BlockSpec((1,) + hd + (PAGE_SIZE,), page(g)) for g in range(DEC_PAGES)]
    return pl.pallas_call(
        functools.partial(_sb_decode_kernel, n_steps=n_steps),
        out_shape=jax.ShapeDtypeStruct((b,) + hd, F32),
        grid_spec=pltpu.PrefetchScalarGridSpec(
            num_scalar_prefetch=1,
            grid=(b, n_steps),
            in_specs=[row_spec, row_spec, row_spec, pl.BlockSpec((N_HEADS, 1), lambda bi, p, pt: (0, 0))]
            + page_specs + page_specs,
            out_specs=row_spec,
            scratch_shapes=[pltpu.VMEM((N_HEADS, 1), F32), pltpu.VMEM(hd + (PAGE_SIZE,), F32),
                            pltpu.VMEM(hd + (PAGE_SIZE,), F32), pltpu.VMEM(hd, F32)],
        ),
        compiler_params=_params(("parallel", "arbitrary")),
        name="sb_attn_decode",
    )(page_table.reshape(-1), q, k_new, v_new, bias.reshape(N_HEADS, 1),
      *([cache_k] * DEC_PAGES), *([cache_v] * DEC_PAGES))


def _head_sum(x, sel, expand):
    return _dot_x2(_dot_x2(x, sel), expand)


def _rwkv_in_kernel(x_ref, prev_ref, g_ref, mix_ref, wr_ref, wk_ref, wv_ref, w0_ref, w1_ref, w2_ref,
                    a0_ref, a1_ref, a2_ref, g1_ref, g2_ref, kk_ref, ka_ref, sel_ref, exp_ref,
                    r_o, lw_o, k_o, v_o, kk_o, a_o, g_o, xn_o):
    xn = _rms(x_ref[...], g_ref[...])
    xx = prev_ref[...] - xn
    mixed = lambda c: (xn + xx * mix_ref[c:c + 1, :]).astype(BF16)
    xr, xw, xk, xv, xa, xg = (mixed(c) for c in range(6))
    r = _dot(xr, wr_ref[...])
    w = w0_ref[...] + _dot(jnp.tanh(_dot(xw, w1_ref[...])).astype(BF16), w2_ref[...])
    w = -_softplus(-w) - 0.5
    k = _dot(xk, wk_ref[...])
    v = _dot(xv, wv_ref[...])
    a = jax.nn.sigmoid(a0_ref[...] + _dot(_dot(xa, a1_ref[...]).astype(BF16), a2_ref[...]))
    g = _dot(jax.nn.sigmoid(_dot(xg, g1_ref[...])).astype(BF16), g2_ref[...])
    kk = k * kk_ref[...]
    nrm = jnp.sqrt(_head_sum(kk * kk, sel_ref[...], exp_ref[...]))
    kk = kk / jnp.maximum(nrm, 1e-12)
    r_o[...] = r
    lw_o[...] = -jnp.exp(w)
    k_o[...] = k * (1.0 + (a - 1.0) * ka_ref[...])
    v_o[...] = v
    kk_o[...] = kk
    a_o[...] = a
    g_o[...] = g
    xn_o[...] = xn


def _head_selectors():
    lane = jnp.arange(D_MODEL)[:, None] // HEAD_DIM
    sel = (lane == jnp.arange(LANES)[None, :]).astype(BF16)
    return sel, sel.T


def rwkv_in(x, x_prev_rows, g, p):
    m, d = x.shape
    tm = _row_tile(m, 256)
    sel, expand = _head_selectors()
    vec = lambda v: v.reshape(1, d)
    ws = [p["mix"], p["w_r"], p["w_k"], p["w_v"], vec(p["w0"]), p["w1"], p["w2"], vec(p["a0"]), p["a1"],
          p["a2"], p["g1"], p["g2"], vec(p["k_k"]), vec(p["k_a"]), sel, expand]
    sd = jax.ShapeDtypeStruct((m, d), F32)
    return pl.pallas_call(
        _rwkv_in_kernel,
        out_shape=(sd,) * 8,
        grid=(m // tm,),
        in_specs=[_row_spec(tm, d), _row_spec(tm, d), _full_spec((1, d))] + [_full_spec(w.shape) for w in ws],
        out_specs=tuple(_row_spec(tm, d) for _ in range(8)),
        compiler_params=_params(("parallel",)),
        name="rwkv_in",
    )(x, x_prev_rows, vec(g), *ws)


def _norm_shift_kernel(x_ref, halo_ref, g_ref, o_ref, *, tm):
    i = pl.program_id(1)
    g = g_ref[...]
    xn = _rms(x_ref[0], g)
    last = _rms(halo_ref[0], g)[7:8, :]
    first = jnp.where(i > 0, last, 0.0)
    rolled = pltpu.roll(xn, 1, axis=0)
    row = lax.broadcasted_iota(jnp.int32, (tm, 1), 0)
    o_ref[0] = jnp.where(row == 0, first, rolled)


def norm_shift(x, g):
    b, s, d = x.shape
    tm = 512
    hb = tm // 8
    return pl.pallas_call(
        functools.partial(_norm_shift_kernel, tm=tm),
        out_shape=jax.ShapeDtypeStruct((b, s, d), F32),
        grid=(b, s // tm),
        in_specs=[
            pl.BlockSpec((1, tm, d), lambda bi, i: (bi, i, 0)),
            pl.BlockSpec((1, 8, d), lambda bi, i: (bi, jnp.maximum(i * hb - 1, 0), 0)),
            pl.BlockSpec((1, d), lambda bi, i: (0, 0)),
        ],
        out_specs=pl.BlockSpec((1, tm, d), lambda bi, i: (bi, i, 0)),
        compiler_params=_params(("parallel", "arbitrary")),
        name="norm_shift",
    )(x, x, g.reshape(1, d))


WKV_CHUNK = 64
WKV_UNROLL = 2

def _wkv_seq_kernel(r_ref, lw_ref, k_ref, v_ref, kk_ref, a_ref, o_ref, s_ref, h_ref, *, tb, nt):
    i = pl.program_id(1)
    c = WKV_CHUNK
    dh = HEAD_DIM

    @pl.when(i == 0)
    def _():
        h_ref[...] = jnp.zeros_like(h_ref)

    row = lax.broadcasted_iota(jnp.int32, (c, c), 0)
    col = lax.broadcasted_iota(jnp.int32, (c, c), 1)
    tri_incl = _ones_where(col <= row)
    eye = row == col
    row2 = lax.broadcasted_iota(jnp.int32, (2 * c, 2 * c), 0)
    col2 = lax.broadcasted_iota(jnp.int32, (2 * c, 2 * c), 1)
    colm = jnp.where(col2 < c, col2, col2 - c)
    gmask = colm < jnp.where(row2 < c, row2, row2 - c + 1)

    def chunk(n, carry):
        heads = [slice(h * dh, (h + 1) * dh) for h in range(N_HEADS)]
        sls, pre = [], []
        for j in range(WKV_UNROLL):
            sl = pl.ds(pl.multiple_of((n * WKV_UNROLL + j) * c, c), c)
            lw = lw_ref[0, sl, :]
            cl = _dot_x3(tri_incl, lw)
            kk = kk_ref[0, sl, :]
            cl_end = cl[c - 1:c, :]
            inv = jnp.exp(-cl)
            rt_f = r_ref[0, sl, :] * jnp.exp(cl)
            b = kk * a_ref[0, sl, :]
            k = k_ref[0, sl, :]
            to_end = jnp.exp(cl_end - cl)
            sls.append(sl)
            pre.append(dict(
                g_end=jnp.exp(cl_end), rt_f=rt_f, rt=rt_f.astype(BF16),
                at=(-kk * jnp.exp(cl - lw)).astype(BF16),
                bt=(b * inv).astype(BF16), kt=(k * inv).astype(BF16),
                be=(b * to_end).astype(BF16),
                ke=(k * to_end).astype(BF16),
                vb=v_ref[0, sl, :].astype(BF16)))
        units = [(p, ls) for p in pre for ls in heads]
        gms = [jnp.where(gmask, _dot_nt(jnp.concatenate([p["at"][:, ls], p["rt"][:, ls]], axis=0),
                                        jnp.concatenate([p["bt"][:, ls], p["kt"][:, ls]], axis=0)), 0.0)
               for p, ls in units]
        mabs = [gm[0:c, 0:c] for gm in gms]
        mvs = [_dot(gm[0:c, c:].astype(BF16), p["vb"][:, ls]) for gm, (p, ls) in zip(gms, units)]
        pws = [m.astype(BF16) for m in mabs]
        tms = [jnp.where(eye, 1.0, m) for m in mabs]
        for _ in range(5):
            pws = [_dot(pw, pw).astype(BF16) for pw in pws]
            tms = [tm_ + _dot(tm_.astype(BF16), pw) for tm_, pw in zip(tms, pws)]
        aws = [_dot(tm_.astype(BF16), jnp.concatenate([p["at"][:, ls], mv.astype(BF16)], axis=1))
               for tm_, mv, (p, ls) in zip(tms, mvs, units)]
        zmats = [jnp.concatenate([aw.astype(BF16),
                                  jnp.concatenate([jnp.zeros((c, dh), BF16), p["vb"][:, ls]], axis=1)], axis=0)
                 for aw, (p, ls) in zip(aws, units)]
        x1s = [_dot(gm[c:, :].astype(BF16), zm) for gm, zm in zip(gms, zmats)]
        x2s = [_dot_tn(jnp.concatenate([p["be"][:, ls], p["ke"][:, ls]], axis=0), zm)
               for zm, (p, ls) in zip(zmats, units)]
        lbs = [jnp.concatenate([p["rt_f"][:, ls] + x1[:, 0:dh],
                                jnp.where(eye, jnp.broadcast_to(p["g_end"][:, ls], (c, dh)), 0.0) + x2[:, 0:dh]],
                               axis=0).astype(BF16)
               for x1, x2, (p, ls) in zip(x1s, x2s, units)]
        for j in range(WKV_UNROLL):
            u0 = j * N_HEADS
            ress = []
            for h in range(N_HEADS):
                hi, lo = _split2(h_ref[h])
                ress.append(_dot(lbs[u0 + h], hi) + _dot(lbs[u0 + h], lo))
            for h, res in enumerate(ress):
                h_ref[h] = res[c:, :] + x2s[u0 + h][:, dh:]
            o_ref[0, sls[j], :] = jnp.concatenate(
                [res[0:c, :] + x1s[u0 + h][:, dh:] for h, res in enumerate(ress)], axis=-1)
        return carry

    lax.fori_loop(0, tb // (c * WKV_UNROLL), chunk, 0)

    @pl.when(i == nt - 1)
    def _():
        s_ref[0] = h_ref[...]


def wkv_seq(r, lw, k, v, kk, a):
    b, s, d = r.shape
    tb = 256
    nt = s // tb
    blk = pl.BlockSpec((1, tb, d), lambda bi, i: (bi, i, 0))
    return pl.pallas_call(
        functools.partial(_wkv_seq_kernel, tb=tb, nt=nt),
        out_shape=(jax.ShapeDtypeStruct((b, s, d), F32),
                   jax.ShapeDtypeStruct((b, N_HEADS, HEAD_DIM, HEAD_DIM), F32)),
        grid=(b, nt),
        in_specs=[blk] * 6,
        out_specs=(blk, pl.BlockSpec((1, N_HEADS, HEAD_DIM, HEAD_DIM), lambda bi, i: (bi, 0, 0, 0))),
        scratch_shapes=[pltpu.VMEM((N_HEADS, HEAD_DIM, HEAD_DIM), F32)],
        compiler_params=_params(("parallel", "arbitrary")),
        name="wkv_seq",
    )(r, lw, k, v, kk, a)


def _wkv_step_kernel(r_ref, lw_ref, k_ref, v_ref, kk_ref, a_ref, s0_ref, o_ref, s_ref, *, tb):
    del tb
    kk = kk_ref[0]
    a_s = -kk
    b_s = kk * a_ref[0]
    w = jnp.exp(lw_ref[0])
    kx = k_ref[0]
    rx = r_ref[0]
    for vi in range(HEAD_DIM):
        s0 = s0_ref[0, vi]
        sa = jnp.sum(s0 * a_s, axis=0, keepdims=True)
        s1 = s0 * w + sa * b_s + v_ref[0, vi:vi + 1, :] * kx
        s_ref[0, vi] = s1
        o_ref[0, vi:vi + 1, :] = jnp.sum(s1 * rx, axis=0, keepdims=True)


def wkv_step(r, lw, k, v, kk, a, s0):
    b = r.shape[-1]
    rows = pl.BlockSpec((1, HEAD_DIM, b), lambda h: (h, 0, 0))
    st = pl.BlockSpec((1, HEAD_DIM, HEAD_DIM, b), lambda h: (h, 0, 0, 0))
    return pl.pallas_call(
        functools.partial(_wkv_step_kernel, tb=b),
        out_shape=(jax.ShapeDtypeStruct((N_HEADS, HEAD_DIM, b), F32), jax.ShapeDtypeStruct(s0.shape, F32)),
        grid=(N_HEADS,),
        in_specs=[rows] * 6 + [st],
        out_specs=(rows, st),
        compiler_params=_params(("parallel",)),
        name="wkv_step",
    )(r, lw, k, v, kk, a, s0)


def _rwkv_out_kernel(o_ref, r_ref, k_ref, v_ref, g_ref, x_ref, rk_ref, lng_ref, lnb_ref, sel_ref, exp_ref,
                     w_ref, y_ref):
    sel, expand = sel_ref[...], exp_ref[...]
    o = o_ref[...]
    mu = _head_sum(o, sel, expand) * (1.0 / HEAD_DIM)
    oc = o - mu
    var = _head_sum(oc * oc, sel, expand) * (1.0 / HEAD_DIM)
    on = oc * lax.rsqrt(var + GN_EPS) * lng_ref[...] + lnb_ref[...]
    bonus = _head_sum(r_ref[...] * k_ref[...] * rk_ref[...], sel, expand)
    on = on + bonus * v_ref[...]
    y_ref[...] = x_ref[...] + _dot((on * g_ref[...]).astype(BF16), w_ref[...])


def rwkv_out(o, r, k, v, g, x, r_k, ln_g, ln_b, w_o):
    m, d = x.shape
    tm = _row_tile(m, 256)
    sel, expand = _head_selectors()
    vec = lambda t: t.reshape(1, d)
    return pl.pallas_call(
        _rwkv_out_kernel,
        out_shape=jax.ShapeDtypeStruct((m, d), F32),
        grid=(m // tm,),
        in_specs=[_row_spec(tm, d)] * 6 + [_full_spec((1, d))] * 3
        + [_full_spec(sel.shape), _full_spec(expand.shape), _full_spec((d, d))],
        out_specs=_row_spec(tm, d),
        compiler_params=_params(("parallel",)),
        name="rwkv_out",
    )(o, r, k, v, g, x, vec(r_k), vec(ln_g), vec(ln_b), sel, expand, w_o)


def _trunk(x, seq, start_pos, conv_buf, pool_buf, kv_past, wkv0, shift0, w):
    b, t, d = x.shape
    m = b * t
    flat = lambda z: z.reshape(m, d)
    x2 = flat(x)

    u = conv_in(x2, w["norm_mix"][0], w["conv_wa"], w["conv_wb"])
    cw = (w["conv_dw"], w["conv_dw_b"], w["conv_ln_g"], w["conv_ln_b"], w["conv_w_out"])
    if seq:
        x2 = flat(conv_seq(u.reshape(b, t, d), x, *cw))
        new_conv = u.reshape(b, t, d)[:, t - (CONV_WIDTH - 1):]
    else:
        x2 = conv_step(u, conv_buf, x2, *cw)
        new_conv = jnp.concatenate([conv_buf[:, 1:], u[:, None]], axis=1)
    x2 = mlp(x2, w["norm_mlp"][0], w["mlp_up"][0], w["mlp_down"][0])

    if seq:
        y, tail = pool_seq(x2.reshape(b, t, d), w["norm_mix"][1], w["pool_w"], w["pool_scale"])
        x2 = flat(y)
        new_pool = tail[:, POOL_HALO - POOL_BUF:]
    else:
        x2, xn = pool_step(x2, pool_buf, start_pos, w["norm_mix"][1], w["pool_w"], w["pool_scale"])
        new_pool = jnp.concatenate([pool_buf[:, 1:], xn[:, None]], axis=1)
    x2 = mlp(x2, w["norm_mlp"][1], w["mlp_up"][1], w["mlp_down"][1])

    if seq:
        q, kt, vt, ktb, vtb = qkv_proj_seq(x2.reshape(b, t, d), w["norm_mix"][2], w["attn_wq"],
                                           w["attn_wk_t"], w["attn_wv_t"])
        att = flat(sb_attn_prompt(q, ktb, vtb, w["attn_sb_bias"]))
        new_k = jnp.transpose(kt.reshape(b, N_HEADS, HEAD_DIM, t), (0, 3, 1, 2))
        new_v = jnp.transpose(vt.reshape(b, N_HEADS, HEAD_DIM, t), (0, 3, 1, 2))
    else:
        q, k, v, _, _ = qkv_proj(x2, w["norm_mix"][2], w["attn_wq"], w["attn_wk"], w["attn_wv"])
        cache_k, cache_v, page_table = kv_past
        hd = lambda z: z.astype(F32).reshape(b, N_HEADS, HEAD_DIM)
        att = sb_attn_decode(hd(q), hd(k), hd(v), w["attn_sb_bias"], cache_k, cache_v, page_table)
        att = att.reshape(b, d)
        new_k = k.reshape(b, t, N_HEADS, HEAD_DIM)
        new_v = v.reshape(b, t, N_HEADS, HEAD_DIM)
    x2 = proj_res(att, x2, w["attn_w_o"])
    x2 = mlp(x2, w["norm_mlp"][2], w["mlp_up"][2], w["mlp_down"][2])

    if seq:
        prev = flat(norm_shift(x2.reshape(b, t, d), w["norm_mix"][3]))
    else:
        prev = shift0
    r, lw, k2, v2, kk, a, g, xn = rwkv_in(x2, prev, w["norm_mix"][3], w["rw"])
    if seq:
        sh = lambda z: z.reshape(b, t, d)
        o, hstate = wkv_seq(sh(r), sh(lw), sh(k2), sh(v2), sh(kk), sh(a))
        o = flat(o)
        new_wkv = jnp.swapaxes(hstate, -1, -2)
        new_shift = xn.reshape(b, t, d)[:, t - 1]
    else:
        hd = lambda z: jnp.transpose(z.reshape(b, N_HEADS, HEAD_DIM), (1, 2, 0))
        o, new_wkv = wkv_step(hd(r), hd(lw), hd(k2), hd(v2), hd(kk), hd(a), jnp.transpose(wkv0, (1, 2, 3, 0)))
        o = jnp.transpose(o, (2, 0, 1)).reshape(b, d)
        new_wkv = jnp.transpose(new_wkv, (3, 0, 1, 2))
        new_shift = xn
    x2 = rwkv_out(o, r, k2, v2, g, x2, w["rw"]["r_k"], w["rw"]["ln_g"], w["rw"]["ln_b"], w["rw"]["w_o"])
    x2 = mlp(x2, w["norm_mlp"][3], w["mlp_up"][3], w["mlp_down"][3], g_final=w["norm_final"])

    return x2.reshape(b, t, d), new_conv, new_pool, new_k, new_v, new_wkv, new_shift


def kernel(x_prompt, x_sample, cache_conv, cache_pool, cache_k, cache_v, page_table, state_wkv, state_shift,
           norm_mix, norm_mlp, norm_final, mlp_w_up, mlp_w_down,
           conv_w_in, conv_dw, conv_dw_b, conv_ln_g, conv_ln_b, conv_w_out,
           pool_w, pool_scale, attn_w_qkv, attn_w_o, attn_sb_bias,
           rw_mix, rw_w_r, rw_w_k, rw_w_v, rw_w_o, rw_w0, rw_w1, rw_w2, rw_a0, rw_a1, rw_a2,
           rw_g1, rw_g2, rw_k_k, rw_k_a, rw_r_k, rw_ln_g, rw_ln_b):
    d = D_MODEL
    bf = lambda z: z.astype(BF16)
    gl = rw_g1.shape[-1]
    glp = -(-gl // LANES) * LANES
    g1 = jnp.pad(rw_g1[0], ((0, 0), (0, glp - gl)))
    g2 = jnp.pad(rw_g2[0], ((0, glp - gl), (0, 0)))
    w = dict(
        norm_mix=norm_mix, norm_mlp=norm_mlp, norm_final=norm_final,
        mlp_up=bf(mlp_w_up), mlp_down=bf(mlp_w_down),
        conv_wa=bf(conv_w_in[0, :, :d]), conv_wb=bf(conv_w_in[0, :, d:]),
        conv_dw=conv_dw[0], conv_dw_b=conv_dw_b[0], conv_ln_g=conv_ln_g[0], conv_ln_b=conv_ln_b[0],
        conv_w_out=bf(conv_w_out[0]),
        pool_w=bf(pool_w[0]), pool_scale=pool_scale[0],
        attn_wq=bf(attn_w_qkv[0, :, :d]), attn_wk=bf(attn_w_qkv[0, :, d:2 * d]),
        attn_wv=bf(attn_w_qkv[0, :, 2 * d:]), attn_w_o=bf(attn_w_o[0]), attn_sb_bias=attn_sb_bias[0],
        attn_wk_t=bf(attn_w_qkv[0, :, d:2 * d].T), attn_wv_t=bf(attn_w_qkv[0, :, 2 * d:].T),
        rw=dict(mix=rw_mix[0], w_r=bf(rw_w_r[0]), w_k=bf(rw_w_k[0]), w_v=bf(rw_w_v[0]), w_o=bf(rw_w_o[0]),
                w0=rw_w0[0], w1=bf(rw_w1[0]), w2=bf(rw_w2[0]), a0=rw_a0[0], a1=bf(rw_a1[0]), a2=bf(rw_a2[0]),
                g1=bf(g1), g2=bf(g2), k_k=rw_k_k[0], k_a=rw_k_a[0], r_k=rw_r_k[0].reshape(-1),
                ln_g=rw_ln_g[0], ln_b=rw_ln_b[0]),
    )
    yp, conv_p, pool_p, k_p, v_p, wkv_p, shift_p = _trunk(
        x_prompt, True, 0, None, None, None, None, None, w)
    past_len = page_table.shape[1] * PAGE_SIZE
    ys, conv_s, pool_s, k_s, v_s, wkv_s, shift_s = _trunk(
        x_sample, False, past_len, cache_conv[0], cache_pool[0],
        (jnp.transpose(cache_k[0], (0, 2, 3, 1)), jnp.transpose(cache_v[0], (0, 2, 3, 1)), page_table),
        state_wkv[0], state_shift[0], w)
    st = lambda z: z[None]
    return (yp, ys, st(conv_p), st(conv_s), st(pool_p), st(pool_s), st(k_p), st(v_p), st(k_s), st(v_s),
            st(wkv_p), st(wkv_s), st(shift_p), st(shift_s))
```

```python
import functools

import jax
import jax.numpy as jnp
from jax import lax
from jax.experimental import pallas as pl
from jax.experimental.pallas import tpu as pltpu

F32 = jnp.float32
BF16 = jnp.bfloat16

D_MODEL = 1024
D_FF = 4 * D_MODEL
HEAD_DIM = 64
N_HEADS = D_MODEL // HEAD_DIM
CONV_WIDTH = 31
POOL_WINDOWS = (2, 4, 8, 16)
POOL_GROUP = D_MODEL // len(POOL_WINDOWS)
POOL_BUF = max(POOL_WINDOWS) - 1
PAGE_SIZE = 128
RMS_EPS = 1e-6
LN_EPS = 1e-5
GN_EPS = 1e-5 * HEAD_DIM
SB_SCALE = HEAD_DIM ** -0.5

LANES = 128
SUBLANES = 8
MXU_DIM = 256
VMEM_LIMIT = 48 << 20


def _params(sem, vmem=VMEM_LIMIT):
    return pltpu.CompilerParams(dimension_semantics=sem, vmem_limit_bytes=vmem)


def _rms(x, g):
    return x * lax.rsqrt(jnp.mean(x * x, axis=-1, keepdims=True) + RMS_EPS) * g


def _dot(a, b):
    return jnp.dot(a, b, preferred_element_type=F32)


def _dot_nt(a, b):
    return lax.dot_general(a, b, (((1,), (1,)), ((), ())), preferred_element_type=F32)


def _dot_tn(a, b):
    return lax.dot_general(a, b, (((0,), (0,)), ((), ())), preferred_element_type=F32)


def _ones_where(cond):
    return jnp.where(cond, 1.0, 0.0).astype(BF16)


def _split2(x):
    hi = x.astype(BF16)
    lo = (x - hi.astype(F32)).astype(BF16)
    return hi, lo


def _dot_x2(x, w):
    hi, lo = _split2(x)
    return _dot(hi, w) + _dot(lo, w)


def _dot_x3(w, x):
    hi = x.astype(BF16)
    r1 = x - hi.astype(F32)
    mid = r1.astype(BF16)
    lo = (r1 - mid.astype(F32)).astype(BF16)
    return _dot(w, hi) + _dot(w, mid) + _dot(w, lo)


def _row_spec(tm, n):
    return pl.BlockSpec((tm, n), lambda i: (i, 0))


def _full_spec(shape):
    return pl.BlockSpec(shape, lambda *_: (0,) * len(shape))


def _row_tile(m, pref):
    return pref if m % pref == 0 else m


def _mlp_kernel(x_ref, g_ref, gf_ref, wu_ref, wd_ref, o_ref, xn_ref, *, nj, final_norm):
    j = pl.program_id(1)

    @pl.when(j == 0)
    def _():
        x = x_ref[...]
        xn_ref[...] = _rms(x, g_ref[...]).astype(BF16)
        o_ref[...] = x

    h = _dot(xn_ref[...], wu_ref[...])
    h = jnp.square(jnp.maximum(h, 0.0)).astype(BF16)
    o_ref[...] += _dot(h, wd_ref[...])

    if final_norm:
        @pl.when(j == nj - 1)
        def _():
            o_ref[...] = _rms(o_ref[...], gf_ref[...])


def mlp(x, g, w_up, w_down, g_final=None):
    m, d = x.shape
    tm = _row_tile(m, 1024)
    tf = 1024
    nj = D_FF // tf
    final_norm = g_final is not None
    gf = g_final if final_norm else g
    return pl.pallas_call(
        functools.partial(_mlp_kernel, nj=nj, final_norm=final_norm),
        out_shape=jax.ShapeDtypeStruct((m, d), F32),
        grid=(m // tm, nj),
        in_specs=[
            pl.BlockSpec((tm, d), lambda i, j: (i, 0)),
            pl.BlockSpec((1, d), lambda i, j: (0, 0)),
            pl.BlockSpec((1, d), lambda i, j: (0, 0)),
            pl.BlockSpec((d, tf), lambda i, j: (0, j)),
            pl.BlockSpec((tf, d), lambda i, j: (j, 0)),
        ],
        out_specs=pl.BlockSpec((tm, d), lambda i, j: (i, 0)),
        scratch_shapes=[pltpu.VMEM((tm, d), BF16)],
        compiler_params=_params(("parallel", "arbitrary")),
        name="mlp",
    )(x, g.reshape(1, d), gf.reshape(1, d), w_up, w_down)


def _proj_res_kernel(a_ref, x_ref, w_ref, o_ref):
    o_ref[...] = x_ref[...] + _dot(a_ref[...].astype(BF16), w_ref[...])


def proj_res(a, x, w):
    m, d = x.shape
    tm = _row_tile(m, 512)
    return pl.pallas_call(
        _proj_res_kernel,
        out_shape=jax.ShapeDtypeStruct((m, d), F32),
        grid=(m // tm,),
        in_specs=[_row_spec(tm, a.shape[1]), _row_spec(tm, d), _full_spec(w.shape)],
        out_specs=_row_spec(tm, d),
        compiler_params=_params(("parallel",)),
        name="proj_res",
    )(a, x, w)


def _conv_in_kernel(x_ref, g_ref, wa_ref, wb_ref, o_ref):
    xn = _rms(x_ref[...], g_ref[...]).astype(BF16)
    o_ref[...] = _dot(xn, wa_ref[...]) * jax.nn.sigmoid(_dot(xn, wb_ref[...]))


def conv_in(x, g, w_a, w_b):
    m, d = x.shape
    tm = _row_tile(m, 512)
    return pl.pallas_call(
        _conv_in_kernel,
        out_shape=jax.ShapeDtypeStruct((m, d), F32),
        grid=(m // tm,),
        in_specs=[_row_spec(tm, d), _full_spec((1, d)), _full_spec(w_a.shape), _full_spec(w_b.shape)],
        out_specs=_row_spec(tm, d),
        compiler_params=_params(("parallel",)),
        name="conv_in",
    )(x, g.reshape(1, d), w_a, w_b)


def _ln_silu_proj(h, lng, lnb, w):
    mu = jnp.mean(h, axis=-1, keepdims=True)
    hc = h - mu
    var = jnp.mean(hc * hc, axis=-1, keepdims=True)
    hn = hc * lax.rsqrt(var + LN_EPS) * lng + lnb
    hn = hn * jax.nn.sigmoid(hn)
    return _dot(hn.astype(BF16), w)


CONV_HALO = 32
CONV_ROWS = 16


def _conv_seq_kernel(u_ref, halo_ref, x_ref, dw_ref, dwb_ref, lng_ref, lnb_ref, w_ref, o_ref,
                     up_ref, h_ref, *, tm):
    i = pl.program_id(1)
    up_ref[0, 0:CONV_HALO, :] = jnp.where(i > 0, halo_ref[0], 0.0)
    up_ref[0, CONV_HALO:, :] = u_ref[0]
    n_sh = tm + CONV_HALO - SUBLANES
    for s in range(1, SUBLANES):
        up_ref[s, 0:n_sh, :] = up_ref[0, s:s + n_sh, :]
    lead = CONV_HALO - (CONV_WIDTH - 1)
    for c in range(tm // CONV_ROWS):
        r0 = c * CONV_ROWS
        acc = jnp.zeros((CONV_ROWS, D_MODEL), F32)
        for j in range(CONV_WIDTH):
            s, a0 = (lead + j) % SUBLANES, r0 + (lead + j) // SUBLANES * SUBLANES
            acc = acc + up_ref[s, a0:a0 + CONV_ROWS, :] * dw_ref[j:j + 1, :]
        h_ref[r0:r0 + CONV_ROWS, :] = acc
    h = h_ref[...] + dwb_ref[...]
    o_ref[0] = x_ref[0] + _ln_silu_proj(h, lng_ref[...], lnb_ref[...], w_ref[...])


def conv_seq(u, x, dw, dwb, lng, lnb, w_out):
    b, s, d = u.shape
    tm = 128
    nt = s // tm
    hb = tm // CONV_HALO
    vec = lambda v: v.reshape(1, d)
    return pl.pallas_call(
        functools.partial(_conv_seq_kernel, tm=tm),
        out_shape=jax.ShapeDtypeStruct((b, s, d), F32),
        grid=(b, nt),
        in_specs=[
            pl.BlockSpec((1, tm, d), lambda bi, i: (bi, i, 0)),
            pl.BlockSpec((1, CONV_HALO, d), lambda bi, i: (bi, jnp.maximum(i * hb - 1, 0), 0)),
            pl.BlockSpec((1, tm, d), lambda bi, i: (bi, i, 0)),
            pl.BlockSpec((CONV_WIDTH, d), lambda bi, i: (0, 0)),
            pl.BlockSpec((1, d), lambda bi, i: (0, 0)),
            pl.BlockSpec((1, d), lambda bi, i: (0, 0)),
            pl.BlockSpec((1, d), lambda bi, i: (0, 0)),
            pl.BlockSpec((d, d), lambda bi, i: (0, 0)),
        ],
        out_specs=pl.BlockSpec((1, tm, d), lambda bi, i: (bi, i, 0)),
        scratch_shapes=[pltpu.VMEM((SUBLANES, tm + CONV_HALO, d), F32), pltpu.VMEM((tm, d), F32)],
        compiler_params=_params(("parallel", "arbitrary")),
        name="conv_seq",
    )(u, u, x, dw, vec(dwb), vec(lng), vec(lnb), w_out)


def _conv_step_kernel(u_ref, buf_ref, x_ref, dw_ref, dwb_ref, lng_ref, lnb_ref, w_ref, o_ref):
    nb = CONV_WIDTH - 1
    h = jnp.sum(buf_ref[...] * dw_ref[0:nb, :][None], axis=1)
    h = h + u_ref[...] * dw_ref[nb:nb + 1, :] + dwb_ref[...]
    o_ref[...] = x_ref[...] + _ln_silu_proj(h, lng_ref[...], lnb_ref[...], w_ref[...])


def conv_step(u, buf, x, dw, dwb, lng, lnb, w_out):
    b, d = u.shape
    tb = 32
    vec = lambda v: v.reshape(1, d)
    return pl.pallas_call(
        _conv_step_kernel,
        out_shape=jax.ShapeDtypeStruct((b, d), F32),
        grid=(b // tb,),
        in_specs=[
            _row_spec(tb, d),
            pl.BlockSpec((tb, CONV_WIDTH - 1, d), lambda i: (i, 0, 0)),
            _row_spec(tb, d),
            _full_spec((CONV_WIDTH, d)), _full_spec((1, d)), _full_spec((1, d)), _full_spec((1, d)),
            _full_spec((d, d)),
        ],
        out_specs=_row_spec(tb, d),
        compiler_params=_params(("parallel",)),
        name="conv_step",
    )(u, buf, x, dw, vec(dwb), vec(lng), vec(lnb), w_out)


POOL_HALO = 16


def _pool_project(d, w_ref, scale):
    ys = []
    for gi in range(len(POOL_WINDOWS)):
        dg = d[:, gi * POOL_GROUP:(gi + 1) * POOL_GROUP].astype(BF16)
        ys.append(_dot(dg, w_ref[gi]))
    return jnp.concatenate(ys, axis=-1) * scale


def _pool_seq_kernel(x_ref, halo_ref, g_ref, w_ref, sc_ref, o_ref, tail_ref, xc_ref, *, tm, nt):
    i = pl.program_id(1)
    g = g_ref[...]
    xn = _rms(x_ref[0], g)
    xc_ref[0:POOL_HALO, :] = jnp.where(i > 0, _rms(halo_ref[0], g), 0.0)
    xc_ref[POOL_HALO:, :] = xn
    pos = i * tm + lax.broadcasted_iota(jnp.int32, (tm, 1), 0)
    means = []
    for gi, w in enumerate(POOL_WINDOWS):
        c = slice(gi * POOL_GROUP, (gi + 1) * POOL_GROUP)
        win = xn[:, c]
        for k in range(1, w):
            win = win + xc_ref[POOL_HALO - k:POOL_HALO - k + tm, c]
        cnt = jnp.minimum(w, pos + 1).astype(F32)
        means.append(win / cnt)
    d = jnp.concatenate(means, axis=-1) - xn
    o_ref[0] = x_ref[0] + _pool_project(d, w_ref, sc_ref[...])

    @pl.when(i == nt - 1)
    def _():
        tail_ref[0] = xn[tm - POOL_HALO:, :]


def pool_seq(x, g, w_grp, scale):
    b, s, d = x.shape
    tm = 256
    nt = s // tm
    hb = tm // POOL_HALO
    return pl.pallas_call(
        functools.partial(_pool_seq_kernel, tm=tm, nt=nt),
        out_shape=(jax.ShapeDtypeStruct((b, s, d), F32), jax.ShapeDtypeStruct((b, POOL_HALO, d), F32)),
        grid=(b, nt),
        in_specs=[
            pl.BlockSpec((1, tm, d), lambda bi, i: (bi, i, 0)),
            pl.BlockSpec((1, POOL_HALO, d), lambda bi, i: (bi, jnp.maximum(i * hb - 1, 0), 0)),
            pl.BlockSpec((1, d), lambda bi, i: (0, 0)),
            pl.BlockSpec(w_grp.shape, lambda bi, i: (0, 0, 0)),
            pl.BlockSpec((1, d), lambda bi, i: (0, 0)),
        ],
        out_specs=(pl.BlockSpec((1, tm, d), lambda bi, i: (bi, i, 0)),
                   pl.BlockSpec((1, POOL_HALO, d), lambda bi, i: (bi, 0, 0))),
        scratch_shapes=[pltpu.VMEM((tm + POOL_HALO, d), F32)],
        compiler_params=_params(("parallel", "arbitrary")),
        name="pool_seq",
    )(x, x, g.reshape(1, d), w_grp, scale.reshape(1, d))


def _pool_step_kernel(x_ref, buf_ref, g_ref, msk_ref, icnt_ref, w_ref, sc_ref, o_ref, xn_ref):
    xn = _rms(x_ref[...], g_ref[...])
    win = xn + jnp.sum(buf_ref[...] * msk_ref[...][None], axis=1)
    d = win * icnt_ref[...] - xn
    o_ref[...] = x_ref[...] + _pool_project(d, w_ref, sc_ref[...])
    xn_ref[...] = xn


def pool_step(x, buf, start_pos, g, w_grp, scale):
    b, d = x.shape
    tb = 32
    rows = jnp.arange(POOL_BUF)[:, None]
    win_of_lane = jnp.repeat(jnp.asarray(POOL_WINDOWS), POOL_GROUP)[None, :]
    msk = (rows >= POOL_BUF - (win_of_lane - 1)).astype(F32)
    icnt = 1.0 / jnp.minimum(win_of_lane, start_pos + 1).astype(F32)
    return pl.pallas_call(
        _pool_step_kernel,
        out_shape=(jax.ShapeDtypeStruct((b, d), F32), jax.ShapeDtypeStruct((b, d), F32)),
        grid=(b // tb,),
        in_specs=[
            _row_spec(tb, d),
            pl.BlockSpec((tb, POOL_BUF, d), lambda i: (i, 0, 0)),
            _full_spec((1, d)), _full_spec((POOL_BUF, d)), _full_spec((1, d)),
            _full_spec(w_grp.shape), _full_spec((1, d)),
        ],
        out_specs=(_row_spec(tb, d), _row_spec(tb, d)),
        compiler_params=_params(("parallel",)),
        name="pool_step",
    )(x, buf, g.reshape(1, d), msk, icnt, w_grp, scale.reshape(1, d))


def _qkv_kernel(x_ref, g_ref, wq_ref, wk_ref, wv_ref, q_ref, k_ref, v_ref, kb_ref, vb_ref):
    xn = _rms(x_ref[...], g_ref[...]).astype(BF16)
    q_ref[...] = (_dot(xn, wq_ref[...]) * (SB_SCALE * LOG2E)).astype(BF16)
    k = _dot(xn, wk_ref[...])
    v = _dot(xn, wv_ref[...])
    k_ref[...] = k
    v_ref[...] = v
    kb_ref[...] = k.astype(BF16)
    vb_ref[...] = v.astype(BF16)


def qkv_proj(x, g, wq, wk, wv):
    m, d = x.shape
    tm = _row_tile(m, 512)
    sd = lambda dt: jax.ShapeDtypeStruct((m, d), dt)
    return pl.pallas_call(
        _qkv_kernel,
        out_shape=(sd(BF16), sd(F32), sd(F32), sd(BF16), sd(BF16)),
        grid=(m // tm,),
        in_specs=[_row_spec(tm, d), _full_spec((1, d)), _full_spec((d, d)), _full_spec((d, d)),
                  _full_spec((d, d))],
        out_specs=tuple(_row_spec(tm, d) for _ in range(5)),
        compiler_params=_params(("parallel",)),
        name="qkv_proj",
    )(x, g.reshape(1, d), wq, wk, wv)


def _qkv_seq_kernel(x_ref, g_ref, wq_ref, wkt_ref, wvt_ref, q_ref, kt_ref, vt_ref, ktb_ref, vtb_ref):
    xn = _rms(x_ref[0], g_ref[...]).astype(BF16)
    q_ref[0] = (_dot(xn, wq_ref[...]) * (SB_SCALE * LOG2E)).astype(BF16)
    kt = _dot_nt(wkt_ref[...], xn)
    vt = _dot_nt(wvt_ref[...], xn)
    kt_ref[0] = kt
    vt_ref[0] = vt
    ktb_ref[0] = kt.astype(BF16)
    vtb_ref[0] = vt.astype(BF16)


def qkv_proj_seq(x, g, wq, wk_t, wv_t):
    b, s, d = x.shape
    tm = 512
    rows = pl.BlockSpec((1, tm, d), lambda bi, i: (bi, i, 0))
    cols = pl.BlockSpec((1, d, tm), lambda bi, i: (bi, 0, i))
    full = lambda shape: pl.BlockSpec(shape, lambda bi, i: (0, 0))
    tsd = lambda dt: jax.ShapeDtypeStruct((b, d, s), dt)
    return pl.pallas_call(
        _qkv_seq_kernel,
        out_shape=(jax.ShapeDtypeStruct((b, s, d), BF16), tsd(F32), tsd(F32), tsd(BF16), tsd(BF16)),
        grid=(b, s // tm),
        in_specs=[rows, full((1, d)), full((d, d)), full((d, d)), full((d, d))],
        out_specs=(rows, cols, cols, cols, cols),
        compiler_params=_params(("parallel", "parallel")),
        name="qkv_proj_seq",
    )(x, g.reshape(1, d), wq, wk_t, wv_t)


def _softplus(z):
    return jnp.maximum(z, 0.0) + jnp.log(1.0 + jnp.exp(-jnp.abs(z)))


ATT_BLK = 256
ATT_Q = 512
ATT_ROWS = 128
ATT_GROUP = 4
LOG2E = 1.4426950408889634


def _softplus2(z):
    neg_abs = lax.bitcast_convert_type(
        lax.bitcast_convert_type(z, jnp.int32) | jnp.int32(-2 ** 31), F32)
    return jnp.maximum(z, 0.0) + jnp.log2(1.0 + jnp.exp2(neg_abs))


def _sb_prompt_kernel(bias_ref, q_ref, k_ref, v_ref, o_ref, z_scr, x_scr, p_scr):
    hp = pl.program_id(1)
    i = pl.program_id(2)
    t = ATT_BLK
    nd = ATT_Q // t
    nr = ATT_Q // ATT_ROWS
    row = lax.broadcasted_iota(jnp.int32, (t, t), 0)
    col = lax.broadcasted_iota(jnp.int32, (t, t), 1)
    neg_upper = jnp.where(row > col, -1.0, 0.0).astype(BF16)
    q_off = lax.broadcasted_iota(jnp.int32, (ATT_ROWS, t), 0)
    k_off = lax.broadcasted_iota(jnp.int32, (ATT_ROWS, t), 1)
    heads = [slice(hh * HEAD_DIM, (hh + 1) * HEAD_DIM) for hh in range(2)]
    rows = [slice(r * ATT_ROWS, (r + 1) * ATT_ROWS) for r in range(nr)]
    chains = [(hh, r) for hh in range(2) for r in range(nr)]
    nc = len(chains)
    bias = [bias_ref[2 * hp + hh] * LOG2E for hh in range(2)]
    q = [q_ref[0, rows[r], heads[hh]] for hh, r in chains]

    def logits(blk, skip=()):
        kb = k_ref[0, :, pl.ds(pl.multiple_of(blk * t, t), t)]
        return [None if r in skip else _dot(q[ci], kb[heads[hh], :]) + bias[hh]
                for ci, (hh, r) in enumerate(chains)]

    def stash_logits(zs):
        for ci in range(nc):
            z_scr[ci] = zs[ci]

    def stash_weights(zs, cs, masks, skip=()):
        cs_new = list(cs)
        for g0 in range(0, nc, ATT_GROUP):
            group = [ci for ci in range(g0, g0 + ATT_GROUP) if chains[ci][1] not in skip]
            sps, sp0 = [], []
            for ci in group:
                r = chains[ci][1]
                sp = _softplus2(zs[ci])
                x_scr[ci] = zs[ci] - sp
                sp = sp if masks[r] is None else jnp.where(masks[r], sp, 0.0)
                sps.append(sp.astype(BF16))
                sp0.append(sp[:, 0:1])
            betweens = [_dot(sp, neg_upper) for sp in sps]
            for ci, bt, s0 in zip(group, betweens, sp0):
                r = chains[ci][1]
                p = jnp.exp2(x_scr[ci] + bt + cs[ci])
                p_scr[ci] = (p if masks[r] is None else jnp.where(masks[r], p, 0.0)).astype(BF16)
                cs_new[ci] = cs[ci] + bt[:, 0:1] - s0
        return cs_new

    def weighted_values(blk, accs, skip=()):
        vb = v_ref[0, :, pl.ds(pl.multiple_of(blk * t, t), t)]
        return [acc if r in skip else acc + _dot_nt(p_scr[ci], vb[heads[hh], :])
                for ci, (acc, (hh, r)) in enumerate(zip(accs, chains))]

    cs = [jnp.zeros((ATT_ROWS, 1), F32)] * nc
    accs = [jnp.zeros((ATT_ROWS, HEAD_DIM), F32)] * nc
    unseen = lambda dblk: [r for r in range(nr) if (r + 1) * ATT_ROWS - 1 <= dblk * t]
    zs = logits(nd * i + nd - 1, unseen(nd - 1))
    for dblk in reversed(range(nd)):
        blk = nd * i + dblk
        z_next = logits(jnp.maximum(blk - 1, 0), unseen(dblk - 1) if dblk else ())
        if dblk < nd - 1:
            accs = weighted_values(blk + 1, accs, unseen(dblk + 1))
        masks = [None if r * ATT_ROWS > dblk * t + t - 1 else (dblk * t + k_off) < (r * ATT_ROWS + q_off)
                 for r in range(nr)]
        cs = stash_weights(zs, cs, masks, unseen(dblk))
        zs = z_next
    stash_logits(zs)

    def body(n, carry):
        cs, accs = carry
        blk = nd * i - 1 - n
        zs = [z_scr[ci] for ci in range(nc)]
        stash_logits(logits(jnp.maximum(blk - 1, 0)))
        accs = weighted_values(blk + 1, accs)
        cs = stash_weights(zs, cs, [None] * nr)
        return cs, accs

    cs, accs = lax.fori_loop(0, nd * i, body, (cs, accs))
    accs = weighted_values(0, accs)
    o_ref[0] = jnp.concatenate(
        [jnp.concatenate([accs[hh * nr + r] for r in range(nr)], axis=0) for hh in range(2)], axis=-1)


def sb_attn_prompt(q, k, v, bias):
    b, s, d = q.shape
    nc = 2 * ATT_Q // ATT_ROWS
    return pl.pallas_call(
        _sb_prompt_kernel,
        out_shape=jax.ShapeDtypeStruct((b, s, d), F32),
        grid=(b, N_HEADS // 2, s // ATT_Q),
        in_specs=[
            pl.BlockSpec(memory_space=pltpu.SMEM),
            pl.BlockSpec((1, ATT_Q, LANES), lambda bi, hp, i: (bi, i, hp)),
            pl.BlockSpec((1, LANES, s), lambda bi, hp, i: (bi, hp, 0)),
            pl.BlockSpec((1, LANES, s), lambda bi, hp, i: (bi, hp, 0)),
        ],
        out_specs=pl.BlockSpec((1, ATT_Q, LANES), lambda bi, hp, i: (bi, i, hp)),
        scratch_shapes=[pltpu.VMEM((nc, ATT_ROWS, ATT_BLK), F32), pltpu.VMEM((nc, ATT_ROWS, ATT_BLK), F32),
                        pltpu.VMEM((nc, ATT_ROWS, ATT_BLK), BF16)],
        compiler_params=_params(("parallel", "parallel", "arbitrary")),
        name="sb_attn_prompt",
    )(bias, q, k, v)


DEC_PAGES = 16


def _sb_decode_kernel(pt_ref, q_ref, kn_ref, vn_ref, bias_ref, *refs, n_steps):
    del pt_ref
    kp_refs, vp_refs = refs[:DEC_PAGES], refs[DEC_PAGES:2 * DEC_PAGES]
    o_ref, c_ref, qcol_ref, acc_ref, onew_ref = refs[2 * DEC_PAGES:]
    p = pl.program_id(1)
    t = PAGE_SIZE
    dh = HEAD_DIM
    bias = bias_ref[...] * LOG2E
    eye = (lax.broadcasted_iota(jnp.int32, (1, dh, dh), 1) == lax.broadcasted_iota(jnp.int32, (1, dh, dh), 2))

    @pl.when(p == 0)
    def _():
        q = q_ref[0]
        zn = jnp.sum(kn_ref[0] * q, axis=-1, keepdims=True) + bias
        q_pos = n_steps * DEC_PAGES * PAGE_SIZE
        visible = (q_pos + lax.broadcasted_iota(jnp.int32, zn.shape, 1)) < q_pos
        spn = _softplus2(zn)
        c_ref[...] = jnp.where(visible, -spn, 0.0)
        onew_ref[...] = jnp.where(visible, jnp.exp2(zn - spn), 0.0) * vn_ref[0]
        acc_ref[...] = jnp.zeros_like(acc_ref)
        q_col = jnp.sum(jnp.where(eye, q[:, None, :], 0.0), axis=-1, keepdims=True)
        qcol_ref[...] = jnp.broadcast_to(q_col, qcol_ref.shape)

    row = lax.broadcasted_iota(jnp.int32, (t, t), 0)
    col = lax.broadcasted_iota(jnp.int32, (t, t), 1)
    neg_upper = jnp.where(row > col, -1.0, 0.0).astype(BF16)
    c = c_ref[...]
    q_col = qcol_ref[...]
    for g in range(DEC_PAGES):
        z = jnp.sum(kp_refs[g][0] * q_col, axis=1) + bias
        sp = _softplus2(z)
        between = _dot(sp.astype(BF16), neg_upper)
        a = jnp.exp2(z - sp + between + c)
        acc_ref[...] += vp_refs[g][0] * a[:, None, :]
        c = c + between[:, 0:1] - sp[:, 0:1]
    c_ref[...] = c

    @pl.when(p == n_steps - 1)
    def _():
        o_col = jnp.sum(acc_ref[...], axis=-1, keepdims=True)
        o_ref[0] = jnp.sum(jnp.where(eye, o_col, 0.0), axis=1) + onew_ref[...]


def sb_attn_decode(q, k_new, v_new, bias, cache_k, cache_v, page_table):
    b = q.shape[0]
    n_pages = page_table.shape[1]
    n_steps = n_pages // DEC_PAGES
    hd = (N_HEADS, HEAD_DIM)

    def page(g):
        return lambda bi, p, pt: (pt[bi * n_pages + (n_pages - 1 - (p * DEC_PAGES + g))], 0, 0, 0)

    row_spec = pl.BlockSpec((1,) + hd, lambda bi, p, pt: (bi, 0, 0))
    page_specs = [pl.BlockSpec((1,) + hd + (PAGE_SIZE,), page(g)) for g in range(DEC_PAGES)]
    return pl.pallas_call(
        functools.partial(_sb_decode_kernel, n_steps=n_steps),
        out_shape=jax.ShapeDtypeStruct((b,) + hd, F32),
        grid_spec=pltpu.PrefetchScalarGridSpec(
            num_scalar_prefetch=1,
            grid=(b, n_steps),
            in_specs=[row_spec, row_spec, row_spec, pl.BlockSpec((N_HEADS, 1), lambda bi, p, pt: (0, 0))]
            + page_specs + page_specs,
            out_specs=row_spec,
            scratch_shapes=[pltpu.VMEM((N_HEADS, 1), F32), pltpu.VMEM(hd + (PAGE_SIZE,), F32),
                            pltpu.VMEM(hd + (PAGE_SIZE,), F32), pltpu.VMEM(hd, F32)],
        ),
        compiler_params=_params(("parallel", "arbitrary")),
        name="sb_attn_decode",
    )(page_table.reshape(-1), q, k_new, v_new, bias.reshape(N_HEADS, 1),
      *([cache_k] * DEC_PAGES), *([cache_v] * DEC_PAGES))


def _head_sum(x, sel, expand):
    return _dot_x2(_dot_x2(x, sel), expand)


def _rwkv_in_kernel(x_ref, prev_ref, g_ref, mix_ref, wr_ref, wk_ref, wv_ref, w0_ref, w1_ref, w2_ref,
                    a0_ref, a1_ref, a2_ref, g1_ref, g2_ref, kk_ref, ka_ref, sel_ref, exp_ref,
                    r_o, lw_o, k_o, v_o, kk_o, a_o, g_o, xn_o):
    xn = _rms(x_ref[...], g_ref[...])
    xx = prev_ref[...] - xn
    mixed = lambda c: (xn + xx * mix_ref[c:c + 1, :]).astype(BF16)
    xr, xw, xk, xv, xa, xg = (mixed(c) for c in range(6))
    r = _dot(xr, wr_ref[...])
    w = w0_ref[...] + _dot(jnp.tanh(_dot(xw, w1_ref[...])).astype(BF16), w2_ref[...])
    w = -_softplus(-w) - 0.5
    k = _dot(xk, wk_ref[...])
    v = _dot(xv, wv_ref[...])
    a = jax.nn.sigmoid(a0_ref[...] + _dot(_dot(xa, a1_ref[...]).astype(BF16), a2_ref[...]))
    g = _dot(jax.nn.sigmoid(_dot(xg, g1_ref[...])).astype(BF16), g2_ref[...])
    kk = k * kk_ref[...]
    nrm = jnp.sqrt(_head_sum(kk * kk, sel_ref[...], exp_ref[...]))
    kk = kk / jnp.maximum(nrm, 1e-12)
    r_o[...] = r
    lw_o[...] = -jnp.exp(w)
    k_o[...] = k * (1.0 + (a - 1.0) * ka_ref[...])
    v_o[...] = v
    kk_o[...] = kk
    a_o[...] = a
    g_o[...] = g
    xn_o[...] = xn


def _head_selectors():
    lane = jnp.arange(D_MODEL)[:, None] // HEAD_DIM
    sel = (lane == jnp.arange(LANES)[None, :]).astype(BF16)
    return sel, sel.T


def rwkv_in(x, x_prev_rows, g, p):
    m, d = x.shape
    tm = _row_tile(m, 256)
    sel, expand = _head_selectors()
    vec = lambda v: v.reshape(1, d)
    ws = [p["mix"], p["w_r"], p["w_k"], p["w_v"], vec(p["w0"]), p["w1"], p["w2"], vec(p["a0"]), p["a1"],
          p["a2"], p["g1"], p["g2"], vec(p["k_k"]), vec(p["k_a"]), sel, expand]
    sd = jax.ShapeDtypeStruct((m, d), F32)
    return pl.pallas_call(
        _rwkv_in_kernel,
        out_shape=(sd,) * 8,
        grid=(m // tm,),
        in_specs=[_row_spec(tm, d), _row_spec(tm, d), _full_spec((1, d))] + [_full_spec(w.shape) for w in ws],
        out_specs=tuple(_row_spec(tm, d) for _ in range(8)),
        compiler_params=_params(("parallel",)),
        name="rwkv_in",
    )(x, x_prev_rows, vec(g), *ws)


def _norm_shift_kernel(x_ref, halo_ref, g_ref, o_ref, *, tm):
    i = pl.program_id(1)
    g = g_ref[...]
    xn = _rms(x_ref[0], g)
    last = _rms(halo_ref[0], g)[7:8, :]
    first = jnp.where(i > 0, last, 0.0)
    rolled = pltpu.roll(xn, 1, axis=0)
    row = lax.broadcasted_iota(jnp.int32, (tm, 1), 0)
    o_ref[0] = jnp.where(row == 0, first, rolled)


def norm_shift(x, g):
    b, s, d = x.shape
    tm = 512
    hb = tm // 8
    return pl.pallas_call(
        functools.partial(_norm_shift_kernel, tm=tm),
        out_shape=jax.ShapeDtypeStruct((b, s, d), F32),
        grid=(b, s // tm),
        in_specs=[
            pl.BlockSpec((1, tm, d), lambda bi, i: (bi, i, 0)),
            pl.BlockSpec((1, 8, d), lambda bi, i: (bi, jnp.maximum(i * hb - 1, 0), 0)),
            pl.BlockSpec((1, d), lambda bi, i: (0, 0)),
        ],
        out_specs=pl.BlockSpec((1, tm, d), lambda bi, i: (bi, i, 0)),
        compiler_params=_params(("parallel", "arbitrary")),
        name="norm_shift",
    )(x, x, g.reshape(1, d))


WKV_CHUNK = 64
WKV_UNROLL = 2

def _wkv_seq_kernel(r_ref, lw_ref, k_ref, v_ref, kk_ref, a_ref, o_ref, s_ref, h_ref, *, tb, nt):
    i = pl.program_id(1)
    c = WKV_CHUNK
    dh = HEAD_DIM

    @pl.when(i == 0)
    def _():
        h_ref[...] = jnp.zeros_like(h_ref)

    row = lax.broadcasted_iota(jnp.int32, (c, c), 0)
    col = lax.broadcasted_iota(jnp.int32, (c, c), 1)
    tri_incl = _ones_where(col <= row)
    eye = row == col
    row2 = lax.broadcasted_iota(jnp.int32, (2 * c, 2 * c), 0)
    col2 = lax.broadcasted_iota(jnp.int32, (2 * c, 2 * c), 1)
    colm = jnp.where(col2 < c, col2, col2 - c)
    gmask = colm < jnp.where(row2 < c, row2, row2 - c + 1)

    def chunk(n, carry):
        heads = [slice(h * dh, (h + 1) * dh) for h in range(N_HEADS)]
        sls, pre = [], []
        for j in range(WKV_UNROLL):
            sl = pl.ds(pl.multiple_of((n * WKV_UNROLL + j) * c, c), c)
            lw = lw_ref[0, sl, :]
            cl = _dot_x3(tri_incl, lw)
            kk = kk_ref[0, sl, :]
            cl_end = cl[c - 1:c, :]
            inv = jnp.exp(-cl)
            rt_f = r_ref[0, sl, :] * jnp.exp(cl)
            b = kk * a_ref[0, sl, :]
            k = k_ref[0, sl, :]
            to_end = jnp.exp(cl_end - cl)
            sls.append(sl)
            pre.append(dict(
                g_end=jnp.exp(cl_end), rt_f=rt_f, rt=rt_f.astype(BF16),
                at=(-kk * jnp.exp(cl - lw)).astype(BF16),
                bt=(b * inv).astype(BF16), kt=(k * inv).astype(BF16),
                be=(b * to_end).astype(BF16),
                ke=(k * to_end).astype(BF16),
                vb=v_ref[0, sl, :].astype(BF16)))
        units = [(p, ls) for p in pre for ls in heads]
        gms = [jnp.where(gmask, _dot_nt(jnp.concatenate([p["at"][:, ls], p["rt"][:, ls]], axis=0),
                                        jnp.concatenate([p["bt"][:, ls], p["kt"][:, ls]], axis=0)), 0.0)
               for p, ls in units]
        mabs = [gm[0:c, 0:c] for gm in gms]
        mvs = [_dot(gm[0:c, c:].astype(BF16), p["vb"][:, ls]) for gm, (p, ls) in zip(gms, units)]
        pws = [m.astype(BF16) for m in mabs]
        tms = [jnp.where(eye, 1.0, m) for m in mabs]
        for _ in range(5):
            pws = [_dot(pw, pw).astype(BF16) for pw in pws]
            tms = [tm_ + _dot(tm_.astype(BF16), pw) for tm_, pw in zip(tms, pws)]
        aws = [_dot(tm_.astype(BF16), jnp.concatenate([p["at"][:, ls], mv.astype(BF16)], axis=1))
               for tm_, mv, (p, ls) in zip(tms, mvs, units)]
        zmats = [jnp.concatenate([aw.astype(BF16),
                                  jnp.concatenate([jnp.zeros((c, dh), BF16), p["vb"][:, ls]], axis=1)], axis=0)
                 for aw, (p, ls) in zip(aws, units)]
        x1s = [_dot(gm[c:, :].astype(BF16), zm) for gm, zm in zip(gms, zmats)]
        x2s = [_dot_tn(jnp.concatenate([p["be"][:, ls], p["ke"][:, ls]], axis=0), zm)
               for zm, (p, ls) in zip(zmats, units)]
        lbs = [jnp.concatenate([p["rt_f"][:, ls] + x1[:, 0:dh],
                                jnp.where(eye, jnp.broadcast_to(p["g_end"][:, ls], (c, dh)), 0.0) + x2[:, 0:dh]],
                               axis=0).astype(BF16)
               for x1, x2, (p, ls) in zip(x1s, x2s, units)]
        for j in range(WKV_UNROLL):
            u0 = j * N_HEADS
            ress = []
            for h in range(N_HEADS):
                hi, lo = _split2(h_ref[h])
                ress.append(_dot(lbs[u0 + h], hi) + _dot(lbs[u0 + h], lo))
            for h, res in enumerate(ress):
                h_ref[h] = res[c:, :] + x2s[u0 + h][:, dh:]
            o_ref[0, sls[j], :] = jnp.concatenate(
                [res[0:c, :] + x1s[u0 + h][:, dh:] for h, res in enumerate(ress)], axis=-1)
        return carry

    lax.fori_loop(0, tb // (c * WKV_UNROLL), chunk, 0)

    @pl.when(i == nt - 1)
    def _():
        s_ref[0] = h_ref[...]


def wkv_seq(r, lw, k, v, kk, a):
    b, s, d = r.shape
    tb = 256
    nt = s // tb
    blk = pl.BlockSpec((1, tb, d), lambda bi, i: (bi, i, 0))
    return pl.pallas_call(
        functools.partial(_wkv_seq_kernel, tb=tb, nt=nt),
        out_shape=(jax.ShapeDtypeStruct((b, s, d), F32),
                   jax.ShapeDtypeStruct((b, N_HEADS, HEAD_DIM, HEAD_DIM), F32)),
        grid=(b, nt),
        in_specs=[blk] * 6,
        out_specs=(blk, pl.BlockSpec((1, N_HEADS, HEAD_DIM, HEAD_DIM), lambda bi, i: (bi, 0, 0, 0))),
        scratch_shapes=[pltpu.VMEM((N_HEADS, HEAD_DIM, HEAD_DIM), F32)],
        compiler_params=_params(("parallel", "arbitrary")),
        name="wkv_seq",
    )(r, lw, k, v, kk, a)


def _wkv_step_kernel(r_ref, lw_ref, k_ref, v_ref, kk_ref, a_ref, s0_ref, o_ref, s_ref, *, tb):
    del tb
    kk = kk_ref[0]
    a_s = -kk
    b_s = kk * a_ref[0]
    w = jnp.exp(lw_ref[0])
    kx = k_ref[0]
    rx = r_ref[0]
    for vi in range(HEAD_DIM):
        s0 = s0_ref[0, vi]
        sa = jnp.sum(s0 * a_s, axis=0, keepdims=True)
        s1 = s0 * w + sa * b_s + v_ref[0, vi:vi + 1, :] * kx
        s_ref[0, vi] = s1
        o_ref[0, vi:vi + 1, :] = jnp.sum(s1 * rx, axis=0, keepdims=True)


def wkv_step(r, lw, k, v, kk, a, s0):
    b = r.shape[-1]
    rows = pl.BlockSpec((1, HEAD_DIM, b), lambda h: (h, 0, 0))
    st = pl.BlockSpec((1, HEAD_DIM, HEAD_DIM, b), lambda h: (h, 0, 0, 0))
    return pl.pallas_call(
        functools.partial(_wkv_step_kernel, tb=b),
        out_shape=(jax.ShapeDtypeStruct((N_HEADS, HEAD_DIM, b), F32), jax.ShapeDtypeStruct(s0.shape, F32)),
        grid=(N_HEADS,),
        in_specs=[rows] * 6 + [st],
        out_specs=(rows, st),
        compiler_params=_params(("parallel",)),
        name="wkv_step",
    )(r, lw, k, v, kk, a, s0)


def _rwkv_out_kernel(o_ref, r_ref, k_ref, v_ref, g_ref, x_ref, rk_ref, lng_ref, lnb_ref, sel_ref, exp_ref,
                     w_ref, y_ref):
    sel, expand = sel_ref[...], exp_ref[...]
    o = o_ref[...]
    mu = _head_sum(o, sel, expand) * (1.0 / HEAD_DIM)
    oc = o - mu
    var = _head_sum(oc * oc, sel, expand) * (1.0 / HEAD_DIM)
    on = oc * lax.rsqrt(var + GN_EPS) * lng_ref[...] + lnb_ref[...]
    bonus = _head_sum(r_ref[...] * k_ref[...] * rk_ref[...], sel, expand)
    on = on + bonus * v_ref[...]
    y_ref[...] = x_ref[...] + _dot((on * g_ref[...]).astype(BF16), w_ref[...])


def rwkv_out(o, r, k, v, g, x, r_k, ln_g, ln_b, w_o):
    m, d = x.shape
    tm = _row_tile(m, 256)
    sel, expand = _head_selectors()
    vec = lambda t: t.reshape(1, d)
    return pl.pallas_call(
        _rwkv_out_kernel,
        out_shape=jax.ShapeDtypeStruct((m, d), F32),
        grid=(m // tm,),
        in_specs=[_row_spec(tm, d)] * 6 + [_full_spec((1, d))] * 3
        + [_full_spec(sel.shape), _full_spec(expand.shape), _full_spec((d, d))],
        out_specs=_row_spec(tm, d),
        compiler_params=_params(("parallel",)),
        name="rwkv_out",
    )(o, r, k, v, g, x, vec(r_k), vec(ln_g), vec(ln_b), sel, expand, w_o)


def _trunk(x, seq, start_pos, conv_buf, pool_buf, kv_past, wkv0, shift0, w):
    b, t, d = x.shape
    m = b * t
    flat = lambda z: z.reshape(m, d)
    x2 = flat(x)

    u = conv_in(x2, w["norm_mix"][0], w["conv_wa"], w["conv_wb"])
    cw = (w["conv_dw"], w["conv_dw_b"], w["conv_ln_g"], w["conv_ln_b"], w["conv_w_out"])
    if seq:
        x2 = flat(conv_seq(u.reshape(b, t, d), x, *cw))
        new_conv = u.reshape(b, t, d)[:, t - (CONV_WIDTH - 1):]
    else:
        x2 = conv_step(u, conv_buf, x2, *cw)
        new_conv = jnp.concatenate([conv_buf[:, 1:], u[:, None]], axis=1)
    x2 = mlp(x2, w["norm_mlp"][0], w["mlp_up"][0], w["mlp_down"][0])

    if seq:
        y, tail = pool_seq(x2.reshape(b, t, d), w["norm_mix"][1], w["pool_w"], w["pool_scale"])
        x2 = flat(y)
        new_pool = tail[:, POOL_HALO - POOL_BUF:]
    else:
        x2, xn = pool_step(x2, pool_buf, start_pos, w["norm_mix"][1], w["pool_w"], w["pool_scale"])
        new_pool = jnp.concatenate([pool_buf[:, 1:], xn[:, None]], axis=1)
    x2 = mlp(x2, w["norm_mlp"][1], w["mlp_up"][1], w["mlp_down"][1])

    if seq:
        q, kt, vt, ktb, vtb = qkv_proj_seq(x2.reshape(b, t, d), w["norm_mix"][2], w["attn_wq"],
                                           w["attn_wk_t"], w["attn_wv_t"])
        att = flat(sb_attn_prompt(q, ktb, vtb, w["attn_sb_bias"]))
        new_k = jnp.transpose(kt.reshape(b, N_HEADS, HEAD_DIM, t), (0, 3, 1, 2))
        new_v = jnp.transpose(vt.reshape(b, N_HEADS, HEAD_DIM, t), (0, 3, 1, 2))
    else:
        q, k, v, _, _ = qkv_proj(x2, w["norm_mix"][2], w["attn_wq"], w["attn_wk"], w["attn_wv"])
        cache_k, cache_v, page_table = kv_past
        hd = lambda z: z.astype(F32).reshape(b, N_HEADS, HEAD_DIM)
        att = sb_attn_decode(hd(q), hd(k), hd(v), w["attn_sb_bias"], cache_k, cache_v, page_table)
        att = att.reshape(b, d)
        new_k = k.reshape(b, t, N_HEADS, HEAD_DIM)
        new_v = v.reshape(b, t, N_HEADS, HEAD_DIM)
    x2 = proj_res(att, x2, w["attn_w_o"])
    x2 = mlp(x2, w["norm_mlp"][2], w["mlp_up"][2], w["mlp_down"][2])

    if seq:
        prev = flat(norm_shift(x2.reshape(b, t, d), w["norm_mix"][3]))
    else:
        prev = shift0
    r, lw, k2, v2, kk, a, g, xn = rwkv_in(x2, prev, w["norm_mix"][3], w["rw"])
    if seq:
        sh = lambda z: z.reshape(b, t, d)
        o, hstate = wkv_seq(sh(r), sh(lw), sh(k2), sh(v2), sh(kk), sh(a))
        o = flat(o)
        new_wkv = jnp.swapaxes(hstate, -1, -2)
        new_shift = xn.reshape(b, t, d)[:, t - 1]
    else:
        hd = lambda z: jnp.transpose(z.reshape(b, N_HEADS, HEAD_DIM), (1, 2, 0))
        o, new_wkv = wkv_step(hd(r), hd(lw), hd(k2), hd(v2), hd(kk), hd(a), jnp.transpose(wkv0, (1, 2, 3, 0)))
        o = jnp.transpose(o, (2, 0, 1)).reshape(b, d)
        new_wkv = jnp.transpose(new_wkv, (3, 0, 1, 2))
        new_shift = xn
    x2 = rwkv_out(o, r, k2, v2, g, x2, w["rw"]["r_k"], w["rw"]["ln_g"], w["rw"]["ln_b"], w["rw"]["w_o"])
    x2 = mlp(x2, w["norm_mlp"][3], w["mlp_up"][3], w["mlp_down"][3], g_final=w["norm_final"])

    return x2.reshape(b, t, d), new_conv, new_pool, new_k, new_v, new_wkv, new_shift


def kernel(x_prompt, x_sample, cache_conv, cache_pool, cache_k, cache_v, page_table, state_wkv, state_shift,
           norm_mix, norm_mlp, norm_final, mlp_w_up, mlp_w_down,
           conv_w_in, conv_dw, conv_dw_b, conv_ln_g, conv_ln_b, conv_w_out,
           pool_w, pool_scale, attn_w_qkv, attn_w_o, attn_sb_bias,
           rw_mix, rw_w_r, rw_w_k, rw_w_v, rw_w_o, rw_w0, rw_w1, rw_w2, rw_a0, rw_a1, rw_a2,
           rw_g1, rw_g2, rw_k_k, rw_k_a, rw_r_k, rw_ln_g, rw_ln_b):
    d = D_MODEL
    bf = lambda z: z.astype(BF16)
    gl = rw_g1.shape[-1]
    glp = -(-gl // LANES) * LANES
    g1 = jnp.pad(rw_g1[0], ((0, 0), (0, glp - gl)))
    g2 = jnp.pad(rw_g2[0], ((0, glp - gl), (0, 0)))
    w = dict(
        norm_mix=norm_mix, norm_mlp=norm_mlp, norm_final=norm_final,
        mlp_up=bf(mlp_w_up), mlp_down=bf(mlp_w_down),
        conv_wa=bf(conv_w_in[0, :, :d]), conv_wb=bf(conv_w_in[0, :, d:]),
        conv_dw=conv_dw[0], conv_dw_b=conv_dw_b[0], conv_ln_g=conv_ln_g[0], conv_ln_b=conv_ln_b[0],
        conv_w_out=bf(conv_w_out[0]),
        pool_w=bf(pool_w[0]), pool_scale=pool_scale[0],
        attn_wq=bf(attn_w_qkv[0, :, :d]), attn_wk=bf(attn_w_qkv[0, :, d:2 * d]),
        attn_wv=bf(attn_w_qkv[0, :, 2 * d:]), attn_w_o=bf(attn_w_o[0]), attn_sb_bias=attn_sb_bias[0],
        attn_wk_t=bf(attn_w_qkv[0, :, d:2 * d].T), attn_wv_t=bf(attn_w_qkv[0, :, 2 * d:].T),
        rw=dict(mix=rw_mix[0], w_r=bf(rw_w_r[0]), w_k=bf(rw_w_k[0]), w_v=bf(rw_w_v[0]), w_o=bf(rw_w_o[0]),
                w0=rw_w0[0], w1=bf(rw_w1[0]), w2=bf(rw_w2[0]), a0=rw_a0[0], a1=bf(rw_a1[0]), a2=bf(rw_a2[0]),
                g1=bf(g1), g2=bf(g2), k_k=rw_k_k[0], k_a=rw_k_a[0], r_k=rw_r_k[0].reshape(-1),
                ln_g=rw_ln_g[0], ln_b=rw_ln_b[0]),
    )
    yp, conv_p, pool_p, k_p, v_p, wkv_p, shift_p = _trunk(
        x_prompt, True, 0, None, None, None, None, None, w)
    past_len = page_table.shape[1] * PAGE_SIZE
    ys, conv_s, pool_s, k_s, v_s, wkv_s, shift_s = _trunk(
        x_sample, False, past_len, cache_conv[0], cache_pool[0],
        (jnp.transpose(cache_k[0], (0, 2, 3, 1)), jnp.transpose(cache_v[0], (0, 2, 3, 1)), page_table),
        state_wkv[0], state_shift[0], w)
    st = lambda z: z[None]
    return (yp, ys, st(conv_p), st(conv_s), st(pool_p), st(pool_s), st(k_p), st(v_p), st(k_s), st(v_s),
            st(wkv_p), st(wkv_s), st(shift_p), st(shift_s))
```
